```python
import math
import jax
import jax.numpy as jnp
from jax import lax
import numpy as np


D_MODEL = 2048
BATCH = 2
SEQ = 16384
DEPTH = 2

GRID_W = 64
CTX_LEN = 256
D_MIX = D_MODEL
GROUP_W = D_MIX // 4
HEAD_DIM = 64
ROPE_BASE = 10000.0
Q_BLOCK = 128
N_MOD = 9
FFN_DIM = ((8 * D_MODEL // 3 + 255) // 256) * 256

DIFF_HEADS = GROUP_W // (2 * HEAD_DIM)
DIFF_VDIM = 2 * HEAD_DIM

RWKV_HEADS = GROUP_W // HEAD_DIM
RWKV_DECAY_RANK = 64
RWKV_A_RANK = 64
RWKV_GATE_RANK = 128
RWKV_SHIFT = 3
RWKV_DECAY_SCALE = 0.606531
RWKV_LN_EPS = 64e-5

MLA_HEADS = GROUP_W // HEAD_DIM
MLA_NOPE = 64
MLA_ROPE = 32
MLA_V = GROUP_W // MLA_HEADS
MLA_Q_RANK = 384
MLA_KV_RANK = 128

SSD_HEADS = GROUP_W // HEAD_DIM
SSD_GROUPS = 2
SSD_STATE = 128
SSD_CONV = 5
SSD_CHUNK = 128

RWKV_SIZES = (GROUP_W, GROUP_W, GROUP_W, RWKV_DECAY_RANK, RWKV_DECAY_RANK, RWKV_A_RANK, RWKV_A_RANK, RWKV_GATE_RANK)
RWKV_IN = sum(RWKV_SIZES)
SSD_XBC = GROUP_W + 2 * SSD_GROUPS * SSD_STATE
IN_SIZES = (DIFF_HEADS * 2 * HEAD_DIM, DIFF_HEADS * 2 * HEAD_DIM, DIFF_HEADS * DIFF_VDIM, RWKV_IN,
            MLA_Q_RANK, MLA_KV_RANK, MLA_ROPE, GROUP_W, SSD_XBC, 2 * SSD_HEADS)
IN_COLS = sum(IN_SIZES)

kernel_name = 'hybrid_parallel_group_diffusion_block'

F32 = jnp.float32


def split_sizes(t, sizes):
    return jnp.split(t, np.cumsum(sizes)[:-1].tolist(), axis=-1)


def rms_norm(t, g, eps=1e-6):
    tf = t.astype(F32)
    tf = tf * lax.rsqrt(jnp.mean(tf * tf, axis=-1, keepdims=True) + eps)
    return (tf * g.astype(F32)).astype(t.dtype)


def swiglu(h, w_gate, w_up, w_down):
    return (jax.nn.silu(h @ w_gate) * (h @ w_up)) @ w_down


def adaln(t, g, m_ctx, m_lat, j, with_ctx):
    h = rms_norm(t, g)
    shift_l, scale_l = m_lat[:, 3 * j, None], m_lat[:, 3 * j + 1, None]
    if not with_ctx:
        return h * (1 + scale_l) + shift_l
    h_ctx = h[:, :CTX_LEN] * (1 + m_ctx[3 * j + 1]) + m_ctx[3 * j]
    return jnp.concatenate([h_ctx, h[:, CTX_LEN:] * (1 + scale_l) + shift_l], axis=1)


def gated_add(t, y, m_ctx, m_lat, j, with_ctx):
    gate_l = m_lat[:, 3 * j + 2, None]
    if not with_ctx:
        return t + gate_l * y
    return t + jnp.concatenate([m_ctx[3 * j + 2] * y[:, :CTX_LEN], gate_l * y[:, CTX_LEN:]], axis=1)


def to_bwd(t):
    return jnp.concatenate([jnp.flip(t[:, :CTX_LEN], 1), jnp.flip(t[:, CTX_LEN:], 1)], axis=1)


def dwconv(t, w):
    k_w, ch = w.shape
    return lax.conv_general_dilated(t, w[:, None, :].astype(t.dtype), (1,), [(k_w // 2, k_w // 2)],
                                    dimension_numbers=('NWC', 'WIO', 'NWC'), feature_group_count=ch)


def prefix_dwconv(t, w):
    return jnp.concatenate([dwconv(t[:, :CTX_LEN], w), dwconv(t[:, CTX_LEN:], w)], axis=1)


def axial_rope_tables(rows, dim):
    quarter = dim // 4
    inv = ROPE_BASE ** (-jnp.arange(quarter, dtype=F32) / quarter)
    pos_r = jnp.repeat(jnp.arange(rows), GRID_W).astype(F32)
    pos_c = jnp.tile(jnp.arange(GRID_W), rows).astype(F32)
    ang = jnp.concatenate([pos_r[:, None] * inv, pos_c[:, None] * inv], axis=-1)
    return jnp.cos(ang), jnp.sin(ang)


def apply_rope(t, cos, sin):
    shape = (1, cos.shape[0]) + (1,) * (t.ndim - 3) + (cos.shape[1],)
    cos, sin = cos.reshape(shape), sin.reshape(shape)
    t1, t2 = jnp.split(t, 2, axis=-1)
    return jnp.concatenate([t1 * cos - t2 * sin, t2 * cos + t1 * sin], axis=-1).astype(t.dtype)


def rope_latent(t, cos, sin):
    return jnp.concatenate([t[:, :CTX_LEN], apply_rope(t[:, CTX_LEN:], cos, sin)], axis=1)


def block_attention(q, k, v, scale):
    bsz, n_q = q.shape[:2]
    qb = jnp.moveaxis(q.reshape((bsz, n_q // Q_BLOCK, Q_BLOCK) + q.shape[2:]), 1, 0)

    def one_block(q_blk):
        s = jnp.einsum('bqhmd,bkhmd->bhmqk', q_blk, k, preferred_element_type=F32) * scale
        p = jax.nn.softmax(s, axis=-1).astype(v.dtype)
        return jnp.einsum('bhmqk,bkhd->bqhmd', p, v)

    o = lax.map(one_block, qb)
    return jnp.moveaxis(o, 0, 1).reshape((bsz, n_q) + o.shape[3:])


def prefix_attention(q, k, v, scale, want_ctx):
    o_lat = block_attention(q[:, CTX_LEN:], k, v, scale)
    if not want_ctx:
        return o_lat
    o_ctx = block_attention(q[:, :CTX_LEN], k[:, :CTX_LEN], v[:, :CTX_LEN], scale)
    return jnp.concatenate([o_ctx, o_lat], axis=1)


def diff_attention(q, k, v, qk_g, lam, subln_g, lam_init, cos, sin, want_ctx):
    bsz, n, _ = q.shape
    q = rms_norm(q.reshape(bsz, n, DIFF_HEADS, 2, HEAD_DIM), qk_g[0])
    k = rms_norm(k.reshape(bsz, n, DIFF_HEADS, 2, HEAD_DIM), qk_g[1])
    q, k = rope_latent(q, cos, sin), rope_latent(k, cos, sin)
    v = v.reshape(bsz, n, DIFF_HEADS, DIFF_VDIM)
    lam_full = (jnp.exp(jnp.sum(lam[0] * lam[1])) - jnp.exp(jnp.sum(lam[2] * lam[3])) + lam_init).astype(v.dtype)
    o = prefix_attention(q, k, v, HEAD_DIM ** -0.5, want_ctx)
    o = o[..., 0, :] - lam_full * o[..., 1, :]
    o = rms_norm(o, subln_g) * (1.0 - lam_init)
    return o.reshape(bsz, -1, DIFF_HEADS * DIFF_VDIM)


def mla_mixer(c_q, c_kv, k_rope, q_norm_g, kv_norm_g, w_uq, w_ukv, nope_g, rope_g, cos, sin, want_ctx):
    bsz, n, _ = c_q.shape
    q = (rms_norm(c_q, q_norm_g) @ w_uq).reshape(bsz, n, MLA_HEADS, MLA_NOPE + MLA_ROPE)
    kv = (rms_norm(c_kv, kv_norm_g) @ w_ukv).reshape(bsz, n, MLA_HEADS, MLA_NOPE + MLA_V)
    q_nope = rms_norm(q[..., :MLA_NOPE], nope_g[0])
    k_nope = rms_norm(kv[..., :MLA_NOPE], nope_g[1])
    v = kv[..., MLA_NOPE:]
    q_rope = rope_latent(rms_norm(q[..., MLA_NOPE:], rope_g[0]), cos, sin)
    k_rope = rope_latent(rms_norm(k_rope[:, :, None, :], rope_g[1]), cos, sin)
    q = jnp.concatenate([q_nope, q_rope], axis=-1)[:, :, :, None, :]
    k = jnp.concatenate([k_nope, jnp.broadcast_to(k_rope, (bsz, n, MLA_HEADS, MLA_ROPE))], axis=-1)[:, :, :, None, :]
    o = prefix_attention(q, k, v, (MLA_NOPE + MLA_ROPE) ** -0.5, want_ctx)
    return o.reshape(bsz, -1, MLA_HEADS * MLA_V)


def wkv7_scan(r, w, k, v, kk, a):
    n_b, _, n_h, dh = r.shape

    def step(s, inp):
        r_t, w_t, k_t, v_t, kk_t, a_t = inp
        sa = jnp.einsum('nhvk,nhk->nhv', s, kk_t)
        s = s * w_t[:, :, None, :] - sa[..., None] * (kk_t * a_t)[:, :, None, :] + v_t[..., None] * k_t[:, :, None, :]
        return s, jnp.einsum('nhvk,nhk->nhv', s, r_t)

    s0 = jnp.zeros((n_b, n_h, dh, dh), F32)
    _, y = lax.scan(step, s0, tuple(jnp.moveaxis(t, 1, 0) for t in (r, w, k, v, kk, a)))
    return jnp.moveaxis(y, 0, 1)


def rwkv7_mixer(p, shift_w, w0, w_up, a0, a_up, g_up, k_k, k_a, r_k, ln_g, ln_b, want_ctx):
    dtype = p.dtype
    p = prefix_dwconv(p, shift_w)
    r, k, v, wd_f, wd_b, ad_f, ad_b, gd = split_sizes(p, RWKV_SIZES)
    bsz, n, ch = r.shape
    nh, dh = RWKV_HEADS, HEAD_DIM
    w_raw = w0[:, None, None] + jnp.einsum('dblr,drc->dblc', jnp.tanh(jnp.stack([wd_f, wd_b])), w_up)
    decay = jnp.exp(-RWKV_DECAY_SCALE * jax.nn.sigmoid(w_raw.astype(F32)))
    a = jax.nn.sigmoid(a0[:, None, None] + jnp.einsum('dblr,drc->dblc', jnp.stack([ad_f, ad_b]), a_up))
    kk = (k * k_k).astype(F32).reshape(bsz, n, nh, dh)
    kk = kk / jnp.maximum(jnp.sqrt(jnp.sum(kk * kk, axis=-1, keepdims=True)), 1e-12)
    k_dir = k[None] * (1 + (a - 1) * k_a)

    def orient(t):
        t = jnp.broadcast_to(t, (2, bsz, n, ch)).reshape(2, bsz, n, nh, dh)
        return jnp.concatenate([t[0], to_bwd(t[1])], axis=0).astype(F32)

    y = wkv7_scan(orient(r[None]), orient(decay), orient(k_dir), orient(v[None]),
                  orient(kk.reshape(1, bsz, n, ch)), orient(a)).reshape(2, bsz, n, nh, dh)
    y = y[0] + to_bwd(y[1])
    if not want_ctx:
        y, r, k, v, gd = (t[:, CTX_LEN:] for t in (y, r, k, v, gd))
    n_out = y.shape[1]
    mu = jnp.mean(y, axis=-1, keepdims=True)
    var = jnp.mean(jnp.square(y - mu), axis=-1, keepdims=True)
    yn = ((y - mu) * lax.rsqrt(var + RWKV_LN_EPS)).reshape(bsz, n_out, ch) * ln_g + ln_b
    rh, kh, vh = (t.reshape(bsz, n_out, nh, dh) for t in (r, k, v))
    bonus = (jnp.sum(rh * kh * r_k, axis=-1, keepdims=True) * vh).reshape(bsz, n_out, ch)
    g = jax.nn.sigmoid(gd) @ g_up
    return ((yn + bonus) * g).astype(dtype)


def ssd_scan(x, a, bm, cm):
    n_b, n, nh, hp = x.shape
    n_c = n // SSD_CHUNK
    causal = jnp.tril(jnp.ones((SSD_CHUNK, SSD_CHUNK), bool))[None, :, :, None]

    def chunks(t):
        return jnp.moveaxis(t.astype(F32).reshape((n_b, n_c, SSD_CHUNK) + t.shape[2:]), 1, 0)

    def step(h, inp):
        xq, aq, bq, cq = inp
        acum = jnp.cumsum(aq, axis=1)
        seg = acum[:, :, None, :] - acum[:, None, :, :]
        decay = jnp.exp(jnp.where(causal, seg, -jnp.inf))
        scores = jnp.einsum('nlhs,nmhs->nlmh', cq, bq) * decay
        y = jnp.einsum('nlmh,nmhp->nlhp', scores, xq)
        y = y + jnp.einsum('nlhs,nhps->nlhp', cq, h) * jnp.exp(acum)[..., None]
        to_end = jnp.exp(acum[:, -1:, :] - acum)
        h = h * jnp.exp(acum[:, -1, :])[:, :, None, None] + jnp.einsum('nmhs,nmh,nmhp->nhps', bq, to_end, xq)
        return h, y

    h0 = jnp.zeros((n_b, nh, hp, SSD_STATE), F32)
    _, y = lax.scan(step, h0, (chunks(x), chunks(a), chunks(bm), chunks(cm)))
    return jnp.moveaxis(y, 0, 1).reshape(n_b, n, nh, hp)


def ssd_mixer(zg, xbc, dt_raw, conv_w, conv_b, dt_bias, a_log, d_skip, norm_g, want_ctx):
    dtype = xbc.dtype
    xbc = jax.nn.silu(prefix_dwconv(xbc, conv_w) + conv_b)
    xs, bm, cm = split_sizes(xbc, (GROUP_W, SSD_GROUPS * SSD_STATE, SSD_GROUPS * SSD_STATE))
    bsz, n, _ = xs.shape
    nh, hp = SSD_HEADS, HEAD_DIM
    xh = xs.reshape(bsz, n, nh, hp)
    bm = jnp.repeat(bm.reshape(bsz, n, SSD_GROUPS, SSD_STATE), nh // SSD_GROUPS, axis=2)
    cm = jnp.repeat(cm.reshape(bsz, n, SSD_GROUPS, SSD_STATE), nh // SSD_GROUPS, axis=2)
    dt = jnp.moveaxis(jax.nn.softplus(dt_raw.reshape(bsz, n, 2, nh) + dt_bias), 2, 0)
    a_dt = dt * -jnp.exp(a_log)[:, None, None, :]
    x_dt = xh[None] * dt[..., None]

    def orient(t):
        return jnp.concatenate([t[0], to_bwd(t[1])], axis=0)

    y = ssd_scan(orient(x_dt), orient(a_dt), jnp.concatenate([bm, to_bwd(bm)], 0),
                 jnp.concatenate([cm, to_bwd(cm)], 0)).reshape(2, bsz, n, nh, hp)
    y = (y[0] + to_bwd(y[1])).astype(dtype) + d_skip[:, None] * xh
    y = y.reshape(bsz, n, nh * hp)
    if not want_ctx:
        y, zg = y[:, CTX_LEN:], zg[:, CTX_LEN:]
    return rms_norm(y * jax.nn.silu(zg), norm_g)


def setup_inputs(seed: int = 0) -> dict:
    key = jax.random.key(seed)
    keys = jax.random.split(key, 40)

    def nrm(i, shape, scale):
        return scale * jax.random.normal(keys[i], shape, F32)

    def gain(i, shape):
        return 1.0 + nrm(i, shape, 0.02)

    nl, d, gw = DEPTH, D_MODEL, GROUP_W
    dt0 = jnp.exp(jax.random.uniform(keys[36], (nl, 2, SSD_HEADS), F32, math.log(1e-3), math.log(1e-1)))
    return {
        'x': nrm(0, (BATCH, SEQ, d), 1.0),
        'c': nrm(1, (BATCH, d), 1.0),
        'ctx': nrm(2, (BATCH, CTX_LEN, d), 1.0),
        'c_ctx': nrm(3, (d,), 1.0),
        'mod_w': nrm(4, (nl, d, N_MOD * d), 0.5 * d ** -0.5),
        'mod_b': nrm(5, (nl, N_MOD, d), 0.02),
        'norm_g': gain(6, (nl, 3, d)),
        'ffn_w_gate': nrm(7, (nl, 2, d, FFN_DIM), d ** -0.5),
        'ffn_w_up': nrm(8, (nl, 2, d, FFN_DIM), d ** -0.5),
        'ffn_w_down': nrm(9, (nl, 2, FFN_DIM, d), FFN_DIM ** -0.5),
        'w_in': nrm(10, (nl, d, IN_COLS), d ** -0.5),
        'w_out': nrm(11, (nl, D_MIX, d), D_MIX ** -0.5),
        'diff_qk_g': gain(12, (nl, 2, HEAD_DIM)),
        'diff_lambda': nrm(13, (nl, 4, HEAD_DIM), 0.1),
        'diff_subln_g': gain(14, (nl, DIFF_VDIM)),
        'rwkv_shift_w': jnp.array([0.25, 0.5, 0.25], F32)[None, :, None] + nrm(15, (nl, RWKV_SHIFT, RWKV_IN), 0.05),
        'rwkv_w0': nrm(16, (nl, 2, gw), 0.5),
        'rwkv_w_up': nrm(17, (nl, 2, RWKV_DECAY_RANK, gw), 0.1 * RWKV_DECAY_RANK ** -0.5),
        'rwkv_a0': nrm(18, (nl, 2, gw), 0.5),
        'rwkv_a_up': nrm(19, (nl, 2, RWKV_A_RANK, gw), RWKV_A_RANK ** -0.5),
        'rwkv_g_up': nrm(20, (nl, RWKV_GATE_RANK, gw), RWKV_GATE_RANK ** -0.5),
        'rwkv_k_k': 0.85 + nrm(21, (nl, gw), 0.02),
        'rwkv_k_a': gain(22, (nl, gw)),
        'rwkv_r_k': nrm(23, (nl, RWKV_HEADS, HEAD_DIM), 0.1),
        'rwkv_ln_g': gain(24, (nl, gw)),
        'rwkv_ln_b': nrm(25, (nl, gw), 0.02),
        'mla_q_norm_g': gain(26, (nl, MLA_Q_RANK)),
        'mla_kv_norm_g': gain(27, (nl, MLA_KV_RANK)),
        'mla_w_uq': nrm(28, (nl, MLA_Q_RANK, MLA_HEADS * (MLA_NOPE + MLA_ROPE)), MLA_Q_RANK ** -0.5),
        'mla_w_ukv': nrm(29, (nl, MLA_KV_RANK, MLA_HEADS * (MLA_NOPE + MLA_V)), MLA_KV_RANK ** -0.5),
        'mla_nope_g': gain(30, (nl, 2, MLA_NOPE)),
        'mla_rope_g': gain(31, (nl, 2, MLA_ROPE)),
        'ssd_conv_w': nrm(32, (nl, SSD_CONV, SSD_XBC), SSD_CONV ** -0.5),
        'ssd_conv_b': nrm(33, (nl, SSD_XBC), 0.02),
        'ssd_dt_bias': dt0 + jnp.log(-jnp.expm1(-dt0)),
        'ssd_a_log': jnp.log(jax.random.uniform(keys[34], (nl, 2, SSD_HEADS), F32, 1.0, 16.0)),
        'ssd_d': gain(35, (nl, SSD_HEADS)),
        'ssd_norm_g': gain(37, (nl, GROUP_W)),
    }


def reference(x, c, ctx, c_ctx, mod_w, mod_b, norm_g, ffn_w_gate, ffn_w_up, ffn_w_down, w_in, w_out,
              diff_qk_g, diff_lambda, diff_subln_g,
              rwkv_shift_w, rwkv_w0, rwkv_w_up, rwkv_a0, rwkv_a_up, rwkv_g_up, rwkv_k_k, rwkv_k_a, rwkv_r_k,
              rwkv_ln_g, rwkv_ln_b,
              mla_q_norm_g, mla_kv_norm_g, mla_w_uq, mla_w_ukv, mla_nope_g, mla_rope_g,
              ssd_conv_w, ssd_conv_b, ssd_dt_bias, ssd_a_log, ssd_d, ssd_norm_g):
    bsz, n_lat, d = x.shape
    rows = n_lat // GRID_W
    cos_d, sin_d = axial_rope_tables(rows, HEAD_DIM)
    cos_m, sin_m = axial_rope_tables(rows, MLA_ROPE)
    z = jnp.concatenate([ctx, x], axis=1)
    for l in range(DEPTH):
        want_ctx = l < DEPTH - 1
        m_lat = (jax.nn.silu(c) @ mod_w[l]).reshape(bsz, N_MOD, d) + mod_b[l]
        m_ctx = (jax.nn.silu(c_ctx) @ mod_w[l]).reshape(N_MOD, d) + mod_b[l]
        h = adaln(z, norm_g[l, 0], m_ctx, m_lat, 0, True)
        z = gated_add(z, 0.5 * swiglu(h, ffn_w_gate[l, 0], ffn_w_up[l, 0], ffn_w_down[l, 0]), m_ctx, m_lat, 0, True)
        h = adaln(z, norm_g[l, 1], m_ctx, m_lat, 1, True)
        dq, dk, dv, rw, mcq, mckv, mkr, sz, sxbc, sdt = split_sizes(h @ w_in[l], IN_SIZES)
        lam_init = 0.8 - 0.6 * math.exp(-0.3 * l)
        o_diff = diff_attention(dq, dk, dv, diff_qk_g[l], diff_lambda[l], diff_subln_g[l], lam_init,
                                cos_d, sin_d, want_ctx)
        o_rwkv = rwkv7_mixer(rw, rwkv_shift_w[l], rwkv_w0[l], rwkv_w_up[l], rwkv_a0[l], rwkv_a_up[l], rwkv_g_up[l],
                             rwkv_k_k[l], rwkv_k_a[l], rwkv_r_k[l], rwkv_ln_g[l], rwkv_ln_b[l], want_ctx)
        o_mla = mla_mixer(mcq, mckv, mkr, mla_q_norm_g[l], mla_kv_norm_g[l], mla_w_uq[l], mla_w_ukv[l],
                          mla_nope_g[l], mla_rope_g[l], cos_m, sin_m, want_ctx)
        o_ssd = ssd_mixer(sz, sxbc, sdt, ssd_conv_w[l], ssd_conv_b[l], ssd_dt_bias[l], ssd_a_log[l], ssd_d[l],
                          ssd_norm_g[l], want_ctx)
        mix = jnp.concatenate([o_diff, o_rwkv, o_mla, o_ssd], axis=-1) @ w_out[l]
        if not want_ctx:
            z = z[:, CTX_LEN:]
        z = gated_add(z, mix, m_ctx, m_lat, 1, want_ctx)
        h = adaln(z, norm_g[l, 2], m_ctx, m_lat, 2, want_ctx)
        z = gated_add(z, 0.5 * swiglu(h, ffn_w_gate[l, 1], ffn_w_up[l, 1], ffn_w_down[l, 1]), m_ctx, m_lat, 2, want_ctx)
    return z
```

```python
import functools

import jax
import jax.numpy as jnp
import numpy as np
from jax import lax
from jax.experimental import pallas as pl
from jax.experimental.pallas import tpu as pltpu

F32 = jnp.float32
BF16 = jnp.bfloat16

CTX_LEN = 256
GRID_W = 64
HEAD_DIM = 64
ROPE_BASE = 10000.0
N_MOD = 9
GROUP_W = 512
DIFF_HEADS = 4
MLA_HEADS = 8
MLA_NOPE = 64
MLA_ROPE = 32
MLA_Q_RANK = 384
MLA_KV_RANK = 128
RWKV_DECAY_SCALE = 0.606531
RWKV_LN_EPS = 64e-5
RWKV_CHUNK = 64
SSD_HEADS = 8
SSD_STATE = 128
SSD_CHUNK = 128
LOG2E = 1.4426950408889634

LANES = 128
ROW_TILE = CTX_LEN
HALO = 8
ATTN_TK = 640
VMEM_LIMIT = 56 * 1024 * 1024

P_DIFF = 1536
P_RWKV = 1920
P_MLA = 640
P_SSD = 1664
P_OFFS = (0, P_DIFF, P_DIFF + P_RWKV, P_DIFF + P_RWKV + P_MLA, P_DIFF + P_RWKV + P_MLA + P_SSD)


def _dot(a, b):
    return jnp.dot(a, b, preferred_element_type=F32)


def _dot_nt(a, b):
    return lax.dot_general(a, b, (((1,), (1,)), ((), ())), preferred_element_type=F32)


def _bf(x):
    return x.astype(BF16)


def _split3(x):
    x1 = x.astype(BF16)
    r1 = x - x1.astype(F32)
    x2 = r1.astype(BF16)
    r2 = r1 - x2.astype(F32)
    return x1, x2, r2.astype(BF16)


def _dot_exact_lhs(a, x):
    x1, x2, x3 = _split3(x)
    return _dot(a, x1) + _dot(a, x2) + _dot(a, x3)


def _dot_exact_rhs(x, a):
    x1, x2, x3 = _split3(x)
    return _dot(x1, a) + _dot(x2, a) + _dot(x3, a)


def _group_stat(x, s):
    hi = x.astype(BF16)
    lo = (x - hi.astype(F32)).astype(BF16)
    blk = s.shape[0]
    outs = []
    for c in range(x.shape[-1] // blk):
        sl = slice(c * blk, (c + 1) * blk)
        outs.append(_dot(hi[:, sl], s) + _dot(lo[:, sl], s))
    return outs[0] if len(outs) == 1 else jnp.concatenate(outs, axis=-1)


def _swap_halves(x, half):
    n = x.shape[-1]
    lane = lax.broadcasted_iota(jnp.int32, (1, n), 1)
    up = pltpu.roll(x, n - half, axis=1)
    dn = pltpu.roll(x, half, axis=1)
    return jnp.where((lane & (2 * half - 1)) < half, up, dn)


def _silu(x):
    return x * jax.nn.sigmoid(x)


def _is_ctx_rows(tile_idx, tm, ctx_len):
    rows = tile_idx * tm + lax.broadcasted_iota(jnp.int32, (tm, 1), 0)
    return rows < ctx_len


def _mod_row(mctx_ref, mlat_ref, idx, is_ctx):
    return jnp.where(is_ctx, mctx_ref[0, idx:idx + 1, :], mlat_ref[0, idx:idx + 1, :])


def _adaln(z, g, mctx_ref, mlat_ref, j, is_ctx):
    ms = jnp.mean(z * z, axis=-1, keepdims=True)
    h = z * lax.rsqrt(ms + 1e-6) * g
    return h * (1.0 + _mod_row(mctx_ref, mlat_ref, 3 * j + 1, is_ctx)) + _mod_row(mctx_ref, mlat_ref, 3 * j, is_ctx)


def _shifted_rows(x, prev, nxt, shifts):
    tm = x.shape[0]
    ext = jnp.concatenate([prev, x, nxt], axis=0)
    n = tm + 2 * HALO
    out = []
    for s in shifts:
        out.append(x if s == 0 else pltpu.roll(ext, s % n, axis=0)[HALO:HALO + tm])
    return out


def _mod_kernel(c_ref, w_ref, b_ref, o_ref):
    o_ref[...] = _dot(_bf(_silu(c_ref[...])), _bf(w_ref[...])) + b_ref[...]


def _mod(cvec, w, b):
    d, n = w.shape
    tn = 1024
    return pl.pallas_call(
        _mod_kernel,
        grid=(n // tn,),
        in_specs=[pl.BlockSpec((8, d), lambda i: (0, 0)),
                  pl.BlockSpec((d, tn), lambda i: (0, i)),
                  pl.BlockSpec((1, tn), lambda i: (0, i))],
        out_specs=pl.BlockSpec((8, tn), lambda i: (0, i)),
        out_shape=jax.ShapeDtypeStruct((8, n), F32),
        compiler_params=pltpu.CompilerParams(dimension_semantics=("parallel",), vmem_limit_bytes=VMEM_LIMIT),
        name="mod",
    )(cvec, w, b)


def _ffn_kernel(z_ref, mctx_ref, mlat_ref, g_ref, wg_ref, wu_ref, wd_ref, o_ref, h_sc, acc_sc, *, j, ctx_len, tm):
    f = pl.program_id(2)
    is_ctx = _is_ctx_rows(pl.program_id(1), tm, ctx_len)

    @pl.when(f == 0)
    def _():
        h_sc[...] = _bf(_adaln(z_ref[0], g_ref[...], mctx_ref, mlat_ref, j, is_ctx))
        acc_sc[...] = jnp.zeros_like(acc_sc)

    h = h_sc[...]
    act = _silu(_dot(h, wg_ref[...])) * _dot(h, wu_ref[...])
    acc_sc[...] += _dot(_bf(act), wd_ref[...])

    @pl.when(f == pl.num_programs(2) - 1)
    def _():
        gate = _mod_row(mctx_ref, mlat_ref, 3 * j + 2, is_ctx)
        o_ref[0] = z_ref[0] + gate * (0.5 * acc_sc[...])


def _ffn(z, m, g, wg, wu, wd, *, j, ctx_len, tm):
    bsz, n, d = z.shape
    ff = wg.shape[1]
    tf = 512
    return pl.pallas_call(
        functools.partial(_ffn_kernel, j=j, ctx_len=ctx_len, tm=tm),
        grid=(bsz, n // tm, ff // tf),
        in_specs=[pl.BlockSpec((1, tm, d), lambda b, i, f: (b, i, 0)),
                  pl.BlockSpec((1, N_MOD, d), lambda b, i, f: (0, 0, 0)),
                  pl.BlockSpec((1, N_MOD, d), lambda b, i, f: (b + 1, 0, 0)),
                  pl.BlockSpec((1, d), lambda b, i, f: (0, 0)),
                  pl.BlockSpec((d, tf), lambda b, i, f: (0, f)),
                  pl.BlockSpec((d, tf), lambda b, i, f: (0, f)),
                  pl.BlockSpec((tf, d), lambda b, i, f: (f, 0))],
        out_specs=pl.BlockSpec((1, tm, d), lambda b, i, f: (b, i, 0)),
        out_shape=jax.ShapeDtypeStruct(z.shape, F32),
        scratch_shapes=[pltpu.VMEM((tm, d), BF16), pltpu.VMEM((tm, d), F32)],
        compiler_params=pltpu.CompilerParams(dimension_semantics=("parallel", "parallel", "arbitrary"),
                                             vmem_limit_bytes=VMEM_LIMIT),
        name="ffn",
    )(z, m, m, g, wg, wu, wd)


def _inproj_kernel(z_ref, mctx_ref, mlat_ref, g_ref, w_ref, od_ref, or_ref, om_ref, os_ref):
    is_ctx = _is_ctx_rows(pl.program_id(1), ROW_TILE, CTX_LEN)
    h = _bf(_adaln(z_ref[0], g_ref[...], mctx_ref, mlat_ref, 1, is_ctx))
    for o_ref, lo, hi in zip((od_ref, or_ref, om_ref, os_ref), P_OFFS[:-1], P_OFFS[1:]):
        o_ref[0] = _dot(h, w_ref[:, lo:hi])


def _inproj(z, m, g, w):
    bsz, n, d = z.shape
    tm = ROW_TILE
    widths = (P_DIFF, P_RWKV, P_MLA, P_SSD)
    return pl.pallas_call(
        _inproj_kernel,
        grid=(bsz, n // tm),
        in_specs=[pl.BlockSpec((1, tm, d), lambda b, i: (b, i, 0)),
                  pl.BlockSpec((1, N_MOD, d), lambda b, i: (0, 0, 0)),
                  pl.BlockSpec((1, N_MOD, d), lambda b, i: (b + 1, 0, 0)),
                  pl.BlockSpec((1, d), lambda b, i: (0, 0)),
                  pl.BlockSpec(w.shape, lambda b, i: (0, 0), pipeline_mode=pl.Buffered(1))],
        out_specs=[pl.BlockSpec((1, tm, wd), lambda b, i: (b, i, 0)) for wd in widths],
        out_shape=[jax.ShapeDtypeStruct((bsz, n, wd), F32) for wd in widths],
        compiler_params=pltpu.CompilerParams(dimension_semantics=("parallel", "parallel"),
                                             vmem_limit_bytes=VMEM_LIMIT),
        name="inproj",
    )(z, m, m, g, w)


def _diff_prep_kernel(p_ref, g_ref, cos_ref, sin_ref, s_ref, q_ref, k_ref, v_ref):
    s = s_ref[...]
    cos = jnp.concatenate([cos_ref[...]] * 4, axis=1)
    sin = jnp.concatenate([sin_ref[...]] * 4, axis=1)

    def norm_rope(x, g):
        xn = x * lax.rsqrt(_group_stat(x * x, s) + 1e-6) * g
        return xn * cos + _swap_halves(xn, HEAD_DIM // 2) * sin

    q = norm_rope(p_ref[0, :, 0:512], g_ref[0:1, :]) * (HEAD_DIM ** -0.5 * LOG2E)
    k = norm_rope(p_ref[0, :, 512:1024], g_ref[1:2, :])
    q_ref[0] = _bf(q)
    k_ref[0] = _bf(k)
    v_ref[0] = _bf(p_ref[0, :, 1024:1536])


def _diff_prep(p, g, cos, sin, s):
    bsz, n, _ = p.shape
    tm = ROW_TILE
    row = lambda b, i: (b, i, 0)
    const = lambda b, i: (0, 0)
    return pl.pallas_call(
        _diff_prep_kernel,
        grid=(bsz, n // tm),
        in_specs=[pl.BlockSpec((1, tm, P_DIFF), row),
                  pl.BlockSpec(g.shape, const),
                  pl.BlockSpec((tm, LANES), lambda b, i: (i, 0)),
                  pl.BlockSpec((tm, LANES), lambda b, i: (i, 0)),
                  pl.BlockSpec(s.shape, const)],
        out_specs=[pl.BlockSpec((1, tm, GROUP_W), row)] * 3,
        out_shape=[jax.ShapeDtypeStruct((bsz, n, GROUP_W), BF16)] * 3,
        compiler_params=pltpu.CompilerParams(dimension_semantics=("parallel", "parallel"),
                                             vmem_limit_bytes=VMEM_LIMIT),
        name="diff_prep",
    )(p, g, cos, sin, s)


def _mla_prep_kernel(p_ref, qng_ref, kvng_ref, wuq_ref, wk_ref, wv_ref, gq_ref, gk_ref, gkr_ref, place_ref, s_ref,
                     cosa_ref, sina_ref, cosb_ref, sinb_ref, q_ref, k_ref, v_ref):
    s = s_ref[...]
    half = MLA_ROPE // 2

    def rms(x, width):
        return x * lax.rsqrt(jnp.sum(x * x, axis=-1, keepdims=True) * (1.0 / width) + 1e-6)

    cqn = _bf(rms(p_ref[0, :, 0:MLA_Q_RANK], MLA_Q_RANK) * qng_ref[...])
    q = _dot(cqn, wuq_ref[...])
    cosa = jnp.concatenate([cosa_ref[...]] * MLA_HEADS, axis=1)
    sina = jnp.concatenate([sina_ref[...]] * MLA_HEADS, axis=1)
    qn = q * lax.rsqrt(_group_stat(q * q, s) + 1e-6) * gq_ref[...]
    qr = (qn * cosa + _swap_halves(qn, half) * sina) * ((MLA_NOPE + MLA_ROPE) ** -0.5 * LOG2E)
    q_ref[0] = _bf(qr)

    ckvn = _bf(rms(p_ref[0, :, MLA_Q_RANK:MLA_Q_RANK + MLA_KV_RANK], MLA_KV_RANK) * kvng_ref[...])
    kn = _dot(ckvn, wk_ref[...])
    knn = kn * lax.rsqrt(_group_stat(kn * kn, s) + 1e-6) * gk_ref[...]
    v_ref[0] = _bf(_dot(ckvn, wv_ref[...]))

    krn = rms(p_ref[0, :, 512:640], MLA_ROPE) * gkr_ref[...]
    krr = krn * cosb_ref[...] + _swap_halves(krn, half) * sinb_ref[...]
    k_ref[0] = _bf(knn + _dot(_bf(krr), place_ref[...]))


def _mla_prep(p, consts, tabs):
    bsz, n, _ = p.shape
    tm = ROW_TILE
    row = lambda b, i: (b, i, 0)
    const = lambda b, i: (0, 0)
    tab = pl.BlockSpec((tm, LANES), lambda b, i: (i, 0))
    qk_w = MLA_HEADS * LANES
    return pl.pallas_call(
        _mla_prep_kernel,
        grid=(bsz, n // tm),
        in_specs=[pl.BlockSpec((1, tm, P_MLA), row)] + [pl.BlockSpec(c.shape, const) for c in consts] + [tab] * 4,
        out_specs=[pl.BlockSpec((1, tm, qk_w), row), pl.BlockSpec((1, tm, qk_w), row),
                   pl.BlockSpec((1, tm, GROUP_W), row)],
        out_shape=[jax.ShapeDtypeStruct((bsz, n, qk_w), BF16), jax.ShapeDtypeStruct((bsz, n, qk_w), BF16),
                   jax.ShapeDtypeStruct((bsz, n, GROUP_W), BF16)],
        compiler_params=pltpu.CompilerParams(dimension_semantics=("parallel", "parallel"),
                                             vmem_limit_bytes=VMEM_LIMIT),
        name="mla_prep",
    )(p, *consts, *tabs)


def _attn_kernel(*refs, mode, n_keys, first_is_ctx, lam_init):
    if mode == "diff":
        q_ref, k_ref, v_ref, lam_ref, g_ref, o_ref = refs
    else:
        q_ref, k_ref, v_ref, o_ref = refs
    q = q_ref[0]
    tq = q.shape[0]
    lo = lax.broadcasted_iota(jnp.int32, (1, LANES), 1) < HEAD_DIM
    if mode == "diff":
        zero = jnp.zeros_like(q)
        qa, qb = jnp.where(lo, q, zero), jnp.where(lo, zero, q)
    else:
        qa, qb = q[:, :LANES], q[:, LANES:]

    def sweep(n_chunks, tk):
        def one(qm, km, vc, m, l, acc):
            s = _dot_nt(qm, km)
            m_new = jnp.maximum(m, jnp.max(s, axis=-1, keepdims=True))
            alpha = jnp.exp2(m - m_new)
            p = jnp.exp2(s - m_new)
            l = alpha * l + jnp.sum(p, axis=-1, keepdims=True)
            acc = alpha * acc + _dot(_bf(p), vc)
            return m_new, l, acc

        def body(c, carry):
            off = pl.multiple_of(c * tk, tk)
            kc = k_ref[0, pl.ds(off, tk), :]
            vc = v_ref[0, pl.ds(off, tk), :]
            ka, kb = (kc, kc) if mode == "diff" else (kc[:, :LANES], kc[:, LANES:])
            return one(qa, ka, vc, *carry[:3]) + one(qb, kb, vc, *carry[3:])

        init = (jnp.full((tq, 1), -1e30, F32), jnp.zeros((tq, 1), F32), jnp.zeros((tq, LANES), F32))
        return lax.fori_loop(0, n_chunks, body, init + init)

    def finish(carry):
        _, la, acca, _, lb, accb = carry
        oa, ob = acca / la, accb / lb
        if mode == "diff":
            lam = lam_ref[...]
            lam_full = (jnp.exp(jnp.sum(lam[0:1] * lam[1:2], axis=-1, keepdims=True))
                        - jnp.exp(jnp.sum(lam[2:3] * lam[3:4], axis=-1, keepdims=True)) + lam_init)
            o = oa - lam_full * ob
            o = o * lax.rsqrt(jnp.mean(o * o, axis=-1, keepdims=True) + 1e-6) * g_ref[...] * (1.0 - lam_init)
        else:
            o = jnp.where(lo, oa, ob)
        o_ref[0] = _bf(o)

    if first_is_ctx:
        i = pl.program_id(2)

        @pl.when(i == 0)
        def _():
            finish(sweep(1, CTX_LEN))

        @pl.when(i > 0)
        def _():
            finish(sweep(n_keys // ATTN_TK, ATTN_TK))
    else:
        finish(sweep(n_keys // ATTN_TK, ATTN_TK))


def _attention(q, k, v, extra, *, mode, want_ctx, lam_init=0.0):
    bsz, n, _ = k.shape
    tq = ROW_TILE
    off = 0 if want_ctx else 1
    n_out = n - off * tq
    qk_w = LANES if mode == "diff" else 2 * LANES
    in_specs = [pl.BlockSpec((1, tq, qk_w), lambda b, h, i: (b, i + off, h)),
                pl.BlockSpec((1, n, qk_w), lambda b, h, i: (b, 0, h)),
                pl.BlockSpec((1, n, LANES), lambda b, h, i: (b, 0, h))]
    in_specs += [pl.BlockSpec(e.shape, lambda b, h, i: (0, 0)) for e in extra]
    return pl.pallas_call(
        functools.partial(_attn_kernel, mode=mode, n_keys=n, first_is_ctx=want_ctx, lam_init=lam_init),
        grid=(bsz, GROUP_W // LANES, n_out // tq),
        in_specs=in_specs,
        out_specs=pl.BlockSpec((1, tq, LANES), lambda b, h, i: (b, i, h)),
        out_shape=jax.ShapeDtypeStruct((bsz, n_out, GROUP_W), BF16),
        compiler_params=pltpu.CompilerParams(dimension_semantics=("parallel", "parallel", "arbitrary"),
                                             vmem_limit_bytes=VMEM_LIMIT),
        name="attn_" + mode,
    )(q, k, v, *extra)


def _halo_valid(i, n_tiles):
    return i >= 2, (i >= 1) & (i < n_tiles - 1)


def _rwkv_prep_kernel(p_ref, prev_ref, next_ref, sw_ref, w0_ref, wup_ref, a0_ref, aup_ref, gup_ref, kkw_ref, ka_ref,
                      ssum_ref, r_ref, k_ref, v_ref, kk_ref, lw_ref, a_ref, kd_ref, g_ref):
    i = pl.program_id(1)
    has_prev, has_next = _halo_valid(i, pl.num_programs(1))
    x = p_ref[0]
    prev = jnp.where(has_prev, prev_ref[0], 0.0)
    nxt = jnp.where(has_next, next_ref[0], 0.0)
    xm1, x0, xp1 = _shifted_rows(x, prev, nxt, (1, 0, -1))
    p = sw_ref[0:1, :] * xm1 + sw_ref[1:2, :] * x0 + sw_ref[2:3, :] * xp1

    r, k, v = p[:, 0:512], p[:, 512:1024], p[:, 1024:1536]
    w_raw = w0_ref[...] + _dot(_bf(jnp.tanh(p[:, 1536:1664])), wup_ref[...])
    lw = -RWKV_DECAY_SCALE * jax.nn.sigmoid(w_raw)
    a = jax.nn.sigmoid(a0_ref[...] + _dot(_bf(p[:, 1664:1792]), aup_ref[...]))
    kk = k * kkw_ref[...]
    kk = kk / jnp.maximum(jnp.sqrt(_group_stat(kk * kk, ssum_ref[...])), 1e-12)
    k2 = jnp.concatenate([k, k], axis=1)
    ka2 = jnp.concatenate([ka_ref[...]] * 2, axis=1)
    r_ref[0] = r
    k_ref[0] = k
    v_ref[0] = v
    kk_ref[0] = kk
    lw_ref[0] = lw
    a_ref[0] = a
    kd_ref[0] = k2 * (1.0 + (a - 1.0) * ka2)
    g_ref[0] = _dot(_bf(jax.nn.sigmoid(p[:, 1792:1920])), gup_ref[...])


def _halo_specs(width, tm):
    per = tm // HALO
    return [pl.BlockSpec((1, tm, width), lambda b, i: (b, i, 0)),
            pl.BlockSpec((1, HALO, width), lambda b, i: (b, jnp.maximum(i * per - 1, 0), 0)),
            pl.BlockSpec((1, HALO, width), lambda b, i: (b, jnp.minimum((i + 1) * per, pl.num_programs(1) * per - 1), 0))]


def _rwkv_prep(p, consts):
    bsz, n, _ = p.shape
    tm = ROW_TILE
    n_t = n // tm
    per = tm // HALO
    row = lambda b, i: (b, i, 0)
    const = lambda b, i: (0, 0)
    in_specs = [pl.BlockSpec((1, tm, P_RWKV), row),
                pl.BlockSpec((1, HALO, P_RWKV), lambda b, i: (b, jnp.maximum(i * per - 1, 0), 0)),
                pl.BlockSpec((1, HALO, P_RWKV), lambda b, i: (b, jnp.minimum((i + 1) * per, n_t * per - 1), 0))]
    in_specs += [pl.BlockSpec(c.shape, const) for c in consts]
    widths = (512, 512, 512, 512, 1024, 1024, 1024, 512)
    return pl.pallas_call(
        _rwkv_prep_kernel,
        grid=(bsz, n_t),
        in_specs=in_specs,
        out_specs=[pl.BlockSpec((1, tm, w), row) for w in widths],
        out_shape=[jax.ShapeDtypeStruct((bsz, n, w), F32) for w in widths],
        compiler_params=pltpu.CompilerParams(dimension_semantics=("parallel", "parallel"),
                                             vmem_limit_bytes=VMEM_LIMIT),
        name="rwkv_prep",
    )(p, p, p, *consts)


def _rwkv_scan_kernel(r_ref, kk_ref, v_ref, lw_ref, a_ref, kd_ref, y_ref, gt_sc):
    c = RWKV_CHUNK
    fwd = pl.program_id(0) == 0

    @pl.when(pl.program_id(2) == 0)
    def _():
        gt_sc[...] = jnp.zeros_like(gt_sc)

    ti = lax.broadcasted_iota(jnp.int32, (c, c), 0)
    si = lax.broadcasted_iota(jnp.int32, (c, c), 1)
    tri = jnp.where(jnp.where(fwd, ti - si, si - ti) >= 0, 1.0, 0.0).astype(BF16)
    t2 = lax.broadcasted_iota(jnp.int32, (2 * c, 2 * c), 0)
    s2 = lax.broadcasted_iota(jnp.int32, (2 * c, 2 * c), 1)
    lag = jnp.where(fwd, t2 - s2, s2 - t2)
    lag = jnp.where((t2 & c) == (s2 & c), lag, -1)
    strict = lag > 0
    incl = lag >= 0
    eye = jnp.where(t2 == s2, 1.0, 0.0)
    m0 = lax.broadcasted_iota(jnp.int32, (1, LANES), 1) < HEAD_DIM

    def stack(x):
        return jnp.concatenate([jnp.where(m0, x, 0.0), jnp.where(m0, 0.0, x)], axis=0)

    for p in range(GROUP_W // LANES):
        sl = slice(p * LANES, (p + 1) * LANES)
        lw = lw_ref[0, :, sl]
        cum = _dot_exact_lhs(tri, lw)
        e_in, e_ex, e_neg = jnp.exp(cum), jnp.exp(cum - lw), jnp.exp(-cum)
        p_end = jnp.exp(jnp.where(fwd, cum[c - 1:c, :], cum[0:1, :]))
        kk = kk_ref[0, :, sl]
        la = stack(kk * e_ex)
        lr = stack(r_ref[0, :, sl] * e_in)
        rb = stack(-(kk * a_ref[0, :, sl]) * e_neg)
        rk = stack(kd_ref[0, :, sl] * e_neg)
        vs = stack(v_ref[0, :, sl])
        la_b, lr_b, rb_b, rk_b, vs_b = _bf(la), _bf(lr), _bf(rb), _bf(rk), _bf(vs)

        a_ab = jnp.where(strict, _dot_nt(la_b, rb_b), 0.0)
        a_ak = jnp.where(strict, _dot_nt(la_b, rk_b), 0.0)
        a_rb = jnp.where(incl, _dot_nt(lr_b, rb_b), 0.0)
        a_rk = jnp.where(incl, _dot_nt(lr_b, rk_b), 0.0)

        tm = eye + a_ab
        apow = a_ab
        for _ in range(5):
            apow_b = _bf(apow)
            apow = _dot(apow_b, apow_b)
            tm = tm + _dot(_bf(tm), _bf(apow))
        tm_b = _bf(tm)
        w = _dot(tm_b, la_b)
        u0 = _dot(tm_b, _bf(_dot(_bf(a_ak), vs_b)))

        gt = gt_sc[p]
        gt_b = _bf(gt)
        u = _dot_nt(_bf(w), gt_b) + u0
        y = _dot_nt(lr_b, gt_b) + _dot(_bf(a_rb), _bf(u)) + _dot(_bf(a_rk), vs_b)
        y_ref[0, 0, :, sl] = y[:c] + y[c:]
        uv = jnp.concatenate([u, vs], axis=0)
        rr = jnp.concatenate([rb * p_end, rk * p_end], axis=0)
        gt_sc[p] = gt * p_end + _dot(_bf(uv.T), _bf(rr))


def _scan_chunk_index(d, s, n_ctx, n_all):
    return jnp.where(d == 0, s, jnp.where(s < n_ctx, n_ctx - 1 - s, n_all + n_ctx - 1 - s))


def _rwkv_scan(r, kk, v, lw, a, kd):
    bsz, n, _ = r.shape
    c = RWKV_CHUNK
    n_all, n_ctx = n // c, CTX_LEN // c
    shared = pl.BlockSpec((1, c, GROUP_W), lambda d, b, s: (b, _scan_chunk_index(d, s, n_ctx, n_all), 0))
    per_dir = pl.BlockSpec((1, c, GROUP_W), lambda d, b, s: (b, _scan_chunk_index(d, s, n_ctx, n_all), d))
    return pl.pallas_call(
        _rwkv_scan_kernel,
        grid=(2, bsz, n_all),
        in_specs=[shared, shared, shared, per_dir, per_dir, per_dir],
        out_specs=pl.BlockSpec((1, 1, c, GROUP_W), lambda d, b, s: (d, b, _scan_chunk_index(d, s, n_ctx, n_all), 0)),
        out_shape=jax.ShapeDtypeStruct((2, bsz, n, GROUP_W), F32),
        scratch_shapes=[pltpu.VMEM((GROUP_W // LANES, LANES, LANES), F32)],
        compiler_params=pltpu.CompilerParams(dimension_semantics=("parallel", "parallel", "arbitrary"),
                                             vmem_limit_bytes=VMEM_LIMIT),
        name="rwkv_scan",
    )(r, kk, v, lw, a, kd)


def _ssd_prep_kernel(p_ref, prev_ref, next_ref, cw_ref, cb_ref, dtb_ref, aneg_ref, e_ref, pa_ref,
                     xs_ref, bc_ref, xdt_ref, adt_ref):
    i = pl.program_id(1)
    has_prev, has_next = _halo_valid(i, pl.num_programs(1))
    x = p_ref[0, :, 512:1536]
    prev = jnp.where(has_prev, prev_ref[0, :, 512:1536], 0.0)
    nxt = jnp.where(has_next, next_ref[0, :, 512:1536], 0.0)
    taps = _shifted_rows(x, prev, nxt, (2, 1, 0, -1, -2))
    conv = cb_ref[...]
    for j, tap in enumerate(taps):
        conv = conv + cw_ref[j:j + 1, :] * tap
    xbc = _silu(conv)
    xs = xbc[:, 0:512]
    xs_ref[0] = xs
    bc_ref[0] = xbc[:, 512:1024]
    t = p_ref[0, :, 1536:1664] + dtb_ref[...]
    dt = jnp.maximum(t, 0.0) + jnp.log(1.0 + jnp.exp(-jnp.abs(t)))
    xdt_ref[0] = jnp.concatenate([xs, xs], axis=1) * _dot_exact_rhs(dt, e_ref[...])
    adt_ref[0] = _dot_exact_rhs(dt * aneg_ref[...], pa_ref[...])


def _ssd_prep(p, consts):
    bsz, n, _ = p.shape
    tm = ROW_TILE
    n_t = n // tm
    per = tm // HALO
    row = lambda b, i: (b, i, 0)
    const = lambda b, i: (0, 0)
    in_specs = [pl.BlockSpec((1, tm, P_SSD), row),
                pl.BlockSpec((1, HALO, P_SSD), lambda b, i: (b, jnp.maximum(i * per - 1, 0), 0)),
                pl.BlockSpec((1, HALO, P_SSD), lambda b, i: (b, jnp.minimum((i + 1) * per, n_t * per - 1), 0))]
    in_specs += [pl.BlockSpec(c.shape, const) for c in consts]
    widths = (512, 512, 1024, 256)
    return pl.pallas_call(
        _ssd_prep_kernel,
        grid=(bsz, n_t),
        in_specs=in_specs,
        out_specs=[pl.BlockSpec((1, tm, w), row) for w in widths],
        out_shape=[jax.ShapeDtypeStruct((bsz, n, w), F32) for w in widths],
        compiler_params=pltpu.CompilerParams(dimension_semantics=("parallel", "parallel"),
                                             vmem_limit_bytes=VMEM_LIMIT),
        name="ssd_prep",
    )(p, p, p, *consts)


def _ssd_scan_kernel(x_ref, a_ref, bc_ref, y_ref, ht_sc):
    q = SSD_CHUNK
    fwd = pl.program_id(0) == 0

    @pl.when(pl.program_id(2) == 0)
    def _():
        ht_sc[...] = jnp.zeros_like(ht_sc)

    ti = lax.broadcasted_iota(jnp.int32, (q, q), 0)
    si = lax.broadcasted_iota(jnp.int32, (q, q), 1)
    tri = jnp.where(fwd, ti - si, si - ti) >= 0
    acum = _dot_exact_lhs(jnp.where(tri, 1.0, 0.0).astype(BF16), a_ref[0])
    acum_t = acum.T
    last = jnp.where(fwd, acum[q - 1:q, :], acum[0:1, :])
    m0 = lax.broadcasted_iota(jnp.int32, (1, LANES), 1) < HEAD_DIM
    bc = bc_ref[0]

    def decay(h):
        seg = acum[:, h:h + 1] - acum_t[h:h + 1, :]
        return jnp.where(tri, jnp.exp(jnp.where(tri, seg, 0.0)), 0.0)

    def per_head(fn, h0):
        return jnp.where(m0, fn(h0), fn(h0 + 1))

    for g in range(2):
        bg = bc[:, g * SSD_STATE:(g + 1) * SSD_STATE]
        cg = bc[:, 256 + g * SSD_STATE:256 + (g + 1) * SSD_STATE]
        cg_b = _bf(cg)
        scores = _dot_nt(cg_b, _bf(bg))
        bg_t = _bf(bg.T)
        for pp in range(2):
            pair = 2 * g + pp
            h0 = 2 * pair
            sl = slice(pair * LANES, (pair + 1) * LANES)
            xp = x_ref[0, :, sl]
            xp_b = _bf(xp)
            y0 = _dot(_bf(scores * decay(h0)), xp_b)
            y1 = _dot(_bf(scores * decay(h0 + 1)), xp_b)
            ht = ht_sc[pair]
            carry = _dot(cg_b, _bf(ht)) * per_head(lambda h: jnp.exp(acum[:, h:h + 1]), h0)
            y_ref[0, 0, :, sl] = jnp.where(m0, y0, y1) + carry
            to_end = per_head(lambda h: jnp.exp(last[:, h:h + 1] - acum[:, h:h + 1]), h0)
            e_last = per_head(lambda h: jnp.exp(last[:, h:h + 1]), h0)
            ht_sc[pair] = ht * e_last + _dot(bg_t, _bf(xp * to_end))


def _ssd_scan(xdt, adt, bc):
    bsz, n, _ = bc.shape
    q = SSD_CHUNK
    n_all, n_ctx = n // q, CTX_LEN // q
    idx = lambda d, s: _scan_chunk_index(d, s, n_ctx, n_all)
    return pl.pallas_call(
        _ssd_scan_kernel,
        grid=(2, bsz, n_all),
        in_specs=[pl.BlockSpec((1, q, GROUP_W), lambda d, b, s: (b, idx(d, s), d)),
                  pl.BlockSpec((1, q, LANES), lambda d, b, s: (b, idx(d, s), d)),
                  pl.BlockSpec((1, q, GROUP_W), lambda d, b, s: (b, idx(d, s), 0))],
        out_specs=pl.BlockSpec((1, 1, q, GROUP_W), lambda d, b, s: (d, b, idx(d, s), 0)),
        out_shape=jax.ShapeDtypeStruct((2, bsz, n, GROUP_W), F32),
        scratch_shapes=[pltpu.VMEM((GROUP_W // LANES, SSD_STATE, LANES), F32)],
        compiler_params=pltpu.CompilerParams(dimension_semantics=("parallel", "parallel", "arbitrary"),
                                             vmem_limit_bytes=VMEM_LIMIT),
        name="ssd_scan",
    )(xdt, adt, bc)


def _post_kernel(z_ref, mctx_ref, mlat_ref, od_ref, om_ref, yrf_ref, yrb_ref, r_ref, k_ref, v_ref, g_ref,
                 lng_ref, lnb_ref, rk_ref, ysf_ref, ysb_ref, xs_ref, zg_ref, dsk_ref, ng_ref, smean_ref, ssum_ref,
                 w_ref, o_ref, *, tile_off):
    is_ctx = _is_ctx_rows(pl.program_id(1) + tile_off, ROW_TILE, CTX_LEN)
    smean = smean_ref[...]
    y = yrf_ref[0, 0] + yrb_ref[0, 0]
    yc = y - _group_stat(y, smean)
    yn = yc * lax.rsqrt(_group_stat(yc * yc, smean) + RWKV_LN_EPS) * lng_ref[...] + lnb_ref[...]
    bonus = _group_stat(r_ref[0] * k_ref[0] * rk_ref[...], ssum_ref[...]) * v_ref[0]
    o_rwkv = (yn + bonus) * g_ref[0]

    t = (ysf_ref[0, 0] + ysb_ref[0, 0] + dsk_ref[...] * xs_ref[0]) * _silu(zg_ref[0])
    o_ssd = t * lax.rsqrt(jnp.mean(t * t, axis=-1, keepdims=True) + 1e-6) * ng_ref[...]

    mix = (_dot(od_ref[0], w_ref[0:512, :]) + _dot(_bf(o_rwkv), w_ref[512:1024, :])
           + _dot(om_ref[0], w_ref[1024:1536, :]) + _dot(_bf(o_ssd), w_ref[1536:2048, :]))
    o_ref[0] = z_ref[0] + _mod_row(mctx_ref, mlat_ref, 5, is_ctx) * mix


def _post(z, m, od, om, yr, r, k, v, g, ys, xs, p_ssd, consts, w, *, want_ctx):
    bsz, n, d = z.shape
    tm = ROW_TILE
    off = 0 if want_ctx else 1
    n_out = n - off * tm
    full = lambda b, i: (b, i + off, 0)
    attn = lambda b, i: (b, i, 0)
    const = lambda b, i: (0, 0)
    seq = lambda width: pl.BlockSpec((1, tm, width), full)
    dirs = lambda dd: pl.BlockSpec((1, 1, tm, GROUP_W), lambda b, i: (dd, b, i + off, 0))
    in_specs = [seq(d),
                pl.BlockSpec((1, N_MOD, d), lambda b, i: (0, 0, 0)),
                pl.BlockSpec((1, N_MOD, d), lambda b, i: (b + 1, 0, 0)),
                pl.BlockSpec((1, tm, GROUP_W), attn), pl.BlockSpec((1, tm, GROUP_W), attn),
                dirs(0), dirs(1), seq(GROUP_W), seq(GROUP_W), seq(GROUP_W), seq(GROUP_W)]
    in_specs += [pl.BlockSpec(c.shape, const) for c in consts[:3]]
    in_specs += [dirs(0), dirs(1), seq(GROUP_W), seq(GROUP_W)]
    in_specs += [pl.BlockSpec(c.shape, const) for c in consts[3:]]
    in_specs += [pl.BlockSpec(w.shape, const, pipeline_mode=pl.Buffered(1))]
    return pl.pallas_call(
        functools.partial(_post_kernel, tile_off=off),
        grid=(bsz, n_out // tm),
        in_specs=in_specs,
        out_specs=pl.BlockSpec((1, tm, d), lambda b, i: (b, i, 0)),
        out_shape=jax.ShapeDtypeStruct((bsz, n_out, d), F32),
        compiler_params=pltpu.CompilerParams(dimension_semantics=("parallel", "parallel"),
                                             vmem_limit_bytes=VMEM_LIMIT),
        name="post",
    )(z, m, m, od, om, yr, yr, r, k, v, g, *consts[:3], ys, ys, xs, p_ssd, *consts[3:], w)


def _pad_cols(w, width):
    return jnp.pad(w, ((0, 0), (0, width - w.shape[1])))


def _block_diag_const(block, reps):
    return jnp.asarray(np.kron(np.eye(reps, dtype=np.float32), block), BF16)


def _group_consts():
    ones64 = np.ones((64, 64), np.float32)
    mla_blk = np.zeros((LANES, LANES), np.float32)
    mla_blk[:MLA_NOPE, :MLA_NOPE] = 1.0 / MLA_NOPE
    mla_blk[MLA_NOPE:MLA_NOPE + MLA_ROPE, MLA_NOPE:MLA_NOPE + MLA_ROPE] = 1.0 / MLA_ROPE
    return (_block_diag_const(ones64 / 64.0, 4), _block_diag_const(ones64, 4), _block_diag_const(mla_blk, 2))


def _rope_angles(n_lat, dim):
    quarter = dim // 4
    inv = ROPE_BASE ** (-jnp.arange(quarter, dtype=F32) / quarter)
    rows = n_lat // GRID_W
    pos_r = jnp.repeat(jnp.arange(rows), GRID_W).astype(F32)
    pos_c = jnp.tile(jnp.arange(GRID_W), rows).astype(F32)
    ang = jnp.concatenate([pos_r[:, None] * inv, pos_c[:, None] * inv], axis=-1)
    return jnp.cos(ang), jnp.sin(ang)


def _rope_lane_tables(n_lat, dim, lane_off):
    cos, sin = _rope_angles(n_lat, dim)
    cos_t = jnp.ones((n_lat, LANES), F32).at[:, lane_off:lane_off + dim].set(jnp.concatenate([cos, cos], axis=-1))
    sin_t = jnp.zeros((n_lat, LANES), F32).at[:, lane_off:lane_off + dim].set(jnp.concatenate([-sin, sin], axis=-1))
    cos_t = jnp.concatenate([jnp.ones((CTX_LEN, LANES), F32), cos_t], axis=0)
    sin_t = jnp.concatenate([jnp.zeros((CTX_LEN, LANES), F32), sin_t], axis=0)
    return cos_t, sin_t


def _row(x):
    return x.reshape(1, -1).astype(F32)


def _mla_consts(q_norm_g, kv_norm_g, w_uq, w_ukv, nope_g, rope_g, s_mla):
    hq = MLA_NOPE + MLA_ROPE
    w_uq = w_uq.reshape(MLA_Q_RANK, MLA_HEADS, hq)
    w_uq = jnp.pad(w_uq, ((0, 0), (0, 0), (0, LANES - hq))).reshape(MLA_Q_RANK, MLA_HEADS * LANES)
    w_ukv = w_ukv.reshape(MLA_KV_RANK, MLA_HEADS, 2 * MLA_NOPE)
    w_k = jnp.pad(w_ukv[:, :, :MLA_NOPE], ((0, 0), (0, 0), (0, LANES - MLA_NOPE))).reshape(MLA_KV_RANK, -1)
    w_v = w_ukv[:, :, MLA_NOPE:].reshape(MLA_KV_RANK, GROUP_W)
    zpad = jnp.zeros((LANES - hq,), F32)
    g_q = jnp.tile(jnp.concatenate([nope_g[0], rope_g[0], zpad]), MLA_HEADS)
    g_k = jnp.tile(jnp.concatenate([nope_g[1], jnp.zeros((LANES - MLA_NOPE,), F32)]), MLA_HEADS)
    g_kr = jnp.concatenate([rope_g[1], jnp.zeros((LANES - MLA_ROPE,), F32)])
    place = np.zeros((LANES, MLA_HEADS * LANES), np.float32)
    for h in range(MLA_HEADS):
        place[np.arange(MLA_ROPE), h * LANES + MLA_NOPE + np.arange(MLA_ROPE)] = 1.0
    return (_row(q_norm_g), _row(kv_norm_g), _bf(w_uq), _bf(w_k), _bf(w_v), _row(g_q), _row(g_k), _row(g_kr),
            jnp.asarray(place, BF16), s_mla)


def _two_dir_block_diag(w):
    z = jnp.zeros_like(w[0])
    return jnp.concatenate([jnp.concatenate([w[0], z], axis=1), jnp.concatenate([z, w[1]], axis=1)], axis=0)


def _ssd_expand_consts():
    e = np.zeros((LANES, 2 * GROUP_W), np.float32)
    pa = np.zeros((LANES, 2 * LANES), np.float32)
    for d in range(2):
        for h in range(SSD_HEADS):
            e[d * SSD_HEADS + h, d * GROUP_W + h * HEAD_DIM:d * GROUP_W + (h + 1) * HEAD_DIM] = 1.0
            pa[d * SSD_HEADS + h, d * LANES + h] = 1.0
    return jnp.asarray(e, BF16), jnp.asarray(pa, BF16)


def _pad_lanes(v):
    return jnp.pad(v.reshape(1, -1).astype(F32), ((0, 0), (0, LANES - v.size)))


def kernel(x, c, ctx, c_ctx, mod_w, mod_b, norm_g, ffn_w_gate, ffn_w_up, ffn_w_down, w_in, w_out, diff_qk_g, diff_lambda, diff_subln_g, rwkv_shift_w, rwkv_w0, rwkv_w_up, rwkv_a0, rwkv_a_up, rwkv_g_up, rwkv_k_k, rwkv_k_a, rwkv_r_k, rwkv_ln_g, rwkv_ln_b, mla_q_norm_g, mla_kv_norm_g, mla_w_uq, mla_w_ukv, mla_nope_g, mla_rope_g, ssd_conv_w, ssd_conv_b, ssd_dt_bias, ssd_a_log, ssd_d, ssd_norm_g):
    bsz, n_lat, d = x.shape
    depth = mod_w.shape[0]
    s_mean64, s_sum64, s_mla = _group_consts()
    e_dt, p_adt = _ssd_expand_consts()
    cos_d, sin_d = _rope_lane_tables(n_lat, HEAD_DIM, 0)
    cos_d = cos_d.at[:, HEAD_DIM:].set(cos_d[:, :HEAD_DIM])
    sin_d = sin_d.at[:, HEAD_DIM:].set(sin_d[:, :HEAD_DIM])
    cos_ma, sin_ma = _rope_lane_tables(n_lat, MLA_ROPE, MLA_NOPE)
    cos_mb, sin_mb = _rope_lane_tables(n_lat, MLA_ROPE, 0)

    cvec = jnp.zeros((8, d), F32).at[0].set(c_ctx).at[1:1 + bsz].set(c)
    z = jnp.concatenate([ctx, x], axis=1)
    tm_ffn = 640

    for l in range(depth):
        want_ctx = l < depth - 1
        lam_init = 0.8 - 0.6 * float(np.exp(-0.3 * l))
        m = _mod(cvec, mod_w[l], mod_b[l].reshape(1, N_MOD * d)).reshape(8, N_MOD, d)

        z = _ffn(z, m, _row(norm_g[l, 0]), _bf(ffn_w_gate[l, 0]), _bf(ffn_w_up[l, 0]), _bf(ffn_w_down[l, 0]),
                 j=0, ctx_len=CTX_LEN, tm=tm_ffn)

        w = w_in[l]
        w_p = jnp.concatenate([w[:, :3456], _pad_cols(w[:, 3456:4000], P_MLA), _pad_cols(w[:, 4000:], P_SSD)], axis=1)
        p_diff, p_rwkv, p_mla, p_ssd = _inproj(z, m, _row(norm_g[l, 1]), _bf(w_p))

        qd, kd, vd = _diff_prep(p_diff, jnp.tile(diff_qk_g[l], (1, 2 * DIFF_HEADS)), cos_d, sin_d, s_mean64)
        o_diff = _attention(qd, kd, vd, (diff_lambda[l], _row(diff_subln_g[l])), mode="diff", want_ctx=want_ctx,
                            lam_init=lam_init)

        qm, km, vm = _mla_prep(p_mla, _mla_consts(mla_q_norm_g[l], mla_kv_norm_g[l], mla_w_uq[l], mla_w_ukv[l],
                                                  mla_nope_g[l], mla_rope_g[l], s_mla),
                               (cos_ma, sin_ma, cos_mb, sin_mb))
        o_mla = _attention(qm, km, vm, (), mode="mla", want_ctx=want_ctx)

        rw_consts = (rwkv_shift_w[l], _row(rwkv_w0[l]), _bf(_two_dir_block_diag(rwkv_w_up[l])), _row(rwkv_a0[l]),
                     _bf(_two_dir_block_diag(rwkv_a_up[l])), _bf(rwkv_g_up[l]), _row(rwkv_k_k[l]),
                     _row(rwkv_k_a[l]), s_sum64)
        r, k, v, kk, lw, a, kdir, g = _rwkv_prep(p_rwkv, rw_consts)
        y_rwkv = _rwkv_scan(r, kk, v, lw, a, kdir)

        sd_consts = (ssd_conv_w[l], _row(ssd_conv_b[l]), _pad_lanes(ssd_dt_bias[l]),
                     _pad_lanes(-jnp.exp(ssd_a_log[l])), e_dt, p_adt)
        xs, bc, xdt, adt = _ssd_prep(p_ssd, sd_consts)
        y_ssd = _ssd_scan(xdt, adt, bc)

        post_consts = (_row(rwkv_ln_g[l]), _row(rwkv_ln_b[l]), _row(rwkv_r_k[l]),
                       _row(jnp.repeat(ssd_d[l], HEAD_DIM)), _row(ssd_norm_g[l]), s_mean64, s_sum64)
        z = _post(z, m, o_diff, o_mla, y_rwkv, r, k, v, g, y_ssd, xs, p_ssd, post_consts, _bf(w_out[l]),
                  want_ctx=want_ctx)

        z = _ffn(z, m, _row(norm_g[l, 2]), _bf(ffn_w_gate[l, 1]), _bf(ffn_w_up[l, 1]), _bf(ffn_w_down[l, 1]),
                 j=2, ctx_len=CTX_LEN if want_ctx else 0, tm=tm_ffn if want_ctx else 512)
    return z
```

```python
import functools

import jax
import jax.numpy as jnp
import numpy as np
from jax import lax
from jax.experimental import pallas as pl
from jax.experimental.pallas import tpu as pltpu

F32 = jnp.float32
BF16 = jnp.bfloat16

CTX_LEN = 256
GRID_W = 64
HEAD_DIM = 64
ROPE_BASE = 10000.0
N_MOD = 9
GROUP_W = 512
DIFF_HEADS = 4
MLA_HEADS = 8
MLA_NOPE = 64
MLA_ROPE = 32
MLA_Q_RANK = 384
MLA_KV_RANK = 128
RWKV_DECAY_SCALE = 0.606531
RWKV_LN_EPS = 64e-5
RWKV_CHUNK = 64
SSD_HEADS = 8
SSD_STATE = 128
SSD_CHUNK = 128
LOG2E = 1.4426950408889634

LANES = 128
ROW_TILE = CTX_LEN
HALO = 8
ATTN_KB = ROW_TILE
ATTN_NB = 5
VMEM_LIMIT = 56 * 1024 * 1024

P_DIFF = 1536
P_RWKV = 1920
P_MLA = 640
P_SSD = 1664
P_OFFS = (0, P_DIFF, P_DIFF + P_RWKV, P_DIFF + P_RWKV + P_MLA, P_DIFF + P_RWKV + P_MLA + P_SSD)


def _dot(a, b):
    return jnp.dot(a, b, preferred_element_type=F32)


def _dot_nt(a, b):
    return lax.dot_general(a, b, (((1,), (1,)), ((), ())), preferred_element_type=F32)


def _bf(x):
    return x.astype(BF16)


def _split3(x):
    x1 = x.astype(BF16)
    r1 = x - x1.astype(F32)
    x2 = r1.astype(BF16)
    r2 = r1 - x2.astype(F32)
    return x1, x2, r2.astype(BF16)


def _dot_exact_lhs(a, x):
    x1, x2, x3 = _split3(x)
    return _dot(a, x1) + _dot(a, x2) + _dot(a, x3)


def _dot_exact_rhs(x, a):
    x1, x2, x3 = _split3(x)
    return _dot(x1, a) + _dot(x2, a) + _dot(x3, a)


def _group_stat(x, s):
    hi = x.astype(BF16)
    lo = (x - hi.astype(F32)).astype(BF16)
    blk = s.shape[0]
    outs = []
    for c in range(x.shape[-1] // blk):
        sl = slice(c * blk, (c + 1) * blk)
        outs.append(_dot(hi[:, sl], s) + _dot(lo[:, sl], s))
    return outs[0] if len(outs) == 1 else jnp.concatenate(outs, axis=-1)


def _swap_halves(x, half):
    n = x.shape[-1]
    lane = lax.broadcasted_iota(jnp.int32, (1, n), 1)
    up = pltpu.roll(x, n - half, axis=1)
    dn = pltpu.roll(x, half, axis=1)
    return jnp.where((lane & (2 * half - 1)) < half, up, dn)


def _silu(x):
    return x * jax.nn.sigmoid(x)


def _is_ctx_rows(tile_idx, tm, ctx_len):
    rows = tile_idx * tm + lax.broadcasted_iota(jnp.int32, (tm, 1), 0)
    return rows < ctx_len


def _mod_row(mctx_ref, mlat_ref, idx, is_ctx):
    return jnp.where(is_ctx, mctx_ref[0, idx:idx + 1, :], mlat_ref[0, idx:idx + 1, :])


def _adaln(z, g, mctx_ref, mlat_ref, j, is_ctx):
    ms = jnp.mean(z * z, axis=-1, keepdims=True)
    h = z * lax.rsqrt(ms + 1e-6) * g
    return h * (1.0 + _mod_row(mctx_ref, mlat_ref, 3 * j + 1, is_ctx)) + _mod_row(mctx_ref, mlat_ref, 3 * j, is_ctx)


def _shifted_rows(x, prev, nxt, shifts):
    tm = x.shape[0]
    ext = jnp.concatenate([prev, x, nxt], axis=0)
    n = tm + 2 * HALO
    out = []
    for s in shifts:
        out.append(x if s == 0 else pltpu.roll(ext, s % n, axis=0)[HALO:HALO + tm])
    return out


def _mod_kernel(c_ref, w_ref, b_ref, o_ref):
    o_ref[...] = _dot(_bf(_silu(c_ref[...])), _bf(w_ref[...])) + b_ref[...]


def _mod(cvec, w, b):
    d, n = w.shape
    tn = 1024
    return pl.pallas_call(
        _mod_kernel,
        grid=(n // tn,),
        in_specs=[pl.BlockSpec((8, d), lambda i: (0, 0)),
                  pl.BlockSpec((d, tn), lambda i: (0, i)),
                  pl.BlockSpec((1, tn), lambda i: (0, i))],
        out_specs=pl.BlockSpec((8, tn), lambda i: (0, i)),
        out_shape=jax.ShapeDtypeStruct((8, n), F32),
        compiler_params=pltpu.CompilerParams(dimension_semantics=("parallel",), vmem_limit_bytes=VMEM_LIMIT),
        name="mod",
    )(cvec, w, b)


def _ffn_kernel(z_ref, mctx_ref, mlat_ref, g_ref, wg_ref, wu_ref, wd_ref, o_ref, h_sc, acc_sc, *, j, ctx_len, tm):
    f = pl.program_id(2)
    is_ctx = _is_ctx_rows(pl.program_id(1), tm, ctx_len)

    @pl.when(f == 0)
    def _():
        h_sc[...] = _bf(_adaln(z_ref[0], g_ref[...], mctx_ref, mlat_ref, j, is_ctx))
        acc_sc[...] = jnp.zeros_like(acc_sc)

    h = h_sc[...]
    act = _silu(_dot(h, wg_ref[...])) * _dot(h, wu_ref[...])
    acc_sc[...] += _dot(_bf(act), wd_ref[...])

    @pl.when(f == pl.num_programs(2) - 1)
    def _():
        gate = _mod_row(mctx_ref, mlat_ref, 3 * j + 2, is_ctx)
        o_ref[0] = z_ref[0] + gate * (0.5 * acc_sc[...])


def _ffn(z, m, g, wg, wu, wd, *, j, ctx_len, tm):
    bsz, n, d = z.shape
    ff = wg.shape[1]
    tf = 512
    return pl.pallas_call(
        functools.partial(_ffn_kernel, j=j, ctx_len=ctx_len, tm=tm),
        grid=(bsz, n // tm, ff // tf),
        in_specs=[pl.BlockSpec((1, tm, d), lambda b, i, f: (b, i, 0)),
                  pl.BlockSpec((1, N_MOD, d), lambda b, i, f: (0, 0, 0)),
                  pl.BlockSpec((1, N_MOD, d), lambda b, i, f: (b + 1, 0, 0)),
                  pl.BlockSpec((1, d), lambda b, i, f: (0, 0)),
                  pl.BlockSpec((d, tf), lambda b, i, f: (0, f)),
                  pl.BlockSpec((d, tf), lambda b, i, f: (0, f)),
                  pl.BlockSpec((tf, d), lambda b, i, f: (f, 0))],
        out_specs=pl.BlockSpec((1, tm, d), lambda b, i, f: (b, i, 0)),
        out_shape=jax.ShapeDtypeStruct(z.shape, F32),
        scratch_shapes=[pltpu.VMEM((tm, d), BF16), pltpu.VMEM((tm, d), F32)],
        compiler_params=pltpu.CompilerParams(dimension_semantics=("parallel", "parallel", "arbitrary"),
                                             vmem_limit_bytes=VMEM_LIMIT),
        name="ffn",
    )(z, m, m, g, wg, wu, wd)


def _inproj_kernel(z_ref, mctx_ref, mlat_ref, g_ref, w_ref, od_ref, or_ref, om_ref, os_ref):
    is_ctx = _is_ctx_rows(pl.program_id(1), ROW_TILE, CTX_LEN)
    h = _bf(_adaln(z_ref[0], g_ref[...], mctx_ref, mlat_ref, 1, is_ctx))
    for o_ref, lo, hi in zip((od_ref, or_ref, om_ref, os_ref), P_OFFS[:-1], P_OFFS[1:]):
        o_ref[0] = _dot(h, w_ref[:, lo:hi])


def _inproj(z, m, g, w):
    bsz, n, d = z.shape
    tm = ROW_TILE
    widths = (P_DIFF, P_RWKV, P_MLA, P_SSD)
    return pl.pallas_call(
        _inproj_kernel,
        grid=(bsz, n // tm),
        in_specs=[pl.BlockSpec((1, tm, d), lambda b, i: (b, i, 0)),
                  pl.BlockSpec((1, N_MOD, d), lambda b, i: (0, 0, 0)),
                  pl.BlockSpec((1, N_MOD, d), lambda b, i: (b + 1, 0, 0)),
                  pl.BlockSpec((1, d), lambda b, i: (0, 0)),
                  pl.BlockSpec(w.shape, lambda b, i: (0, 0), pipeline_mode=pl.Buffered(1))],
        out_specs=[pl.BlockSpec((1, tm, wd), lambda b, i: (b, i, 0)) for wd in widths],
        out_shape=[jax.ShapeDtypeStruct((bsz, n, wd), F32) for wd in widths],
        compiler_params=pltpu.CompilerParams(dimension_semantics=("parallel", "parallel"),
                                             vmem_limit_bytes=VMEM_LIMIT),
        name="inproj",
    )(z, m, m, g, w)


_VT_SPEC = pl.BlockSpec((1, 1, GROUP_W, ROW_TILE), lambda b, i: (b, i, 0, 0))


def _vt_shape(bsz, n):
    return jax.ShapeDtypeStruct((bsz, n // ROW_TILE, GROUP_W, ROW_TILE), BF16)


def _diff_prep_kernel(p_ref, g_ref, cos_ref, sin_ref, s_ref, q_ref, k_ref, vt_ref):
    s = s_ref[...]
    cos = jnp.concatenate([cos_ref[...]] * 4, axis=1)
    sin = jnp.concatenate([sin_ref[...]] * 4, axis=1)

    def norm_rope(x, g):
        xn = x * lax.rsqrt(_group_stat(x * x, s) + 1e-6) * g
        return xn * cos + _swap_halves(xn, HEAD_DIM // 2) * sin

    q = norm_rope(p_ref[0, :, 0:512], g_ref[0:1, :]) * (HEAD_DIM ** -0.5 * LOG2E)
    k = norm_rope(p_ref[0, :, 512:1024], g_ref[1:2, :])
    q_ref[0] = _bf(q)
    k_ref[0] = _bf(k)
    vt_ref[0, 0] = _bf(p_ref[0, :, 1024:1536].T)


def _diff_prep(p, g, cos, sin, s):
    bsz, n, _ = p.shape
    tm = ROW_TILE
    row = lambda b, i: (b, i, 0)
    const = lambda b, i: (0, 0)
    return pl.pallas_call(
        _diff_prep_kernel,
        grid=(bsz, n // tm),
        in_specs=[pl.BlockSpec((1, tm, P_DIFF), row),
                  pl.BlockSpec(g.shape, const),
                  pl.BlockSpec((tm, LANES), lambda b, i: (i, 0)),
                  pl.BlockSpec((tm, LANES), lambda b, i: (i, 0)),
                  pl.BlockSpec(s.shape, const)],
        out_specs=[pl.BlockSpec((1, tm, GROUP_W), row)] * 2 + [_VT_SPEC],
        out_shape=[jax.ShapeDtypeStruct((bsz, n, GROUP_W), BF16)] * 2 + [_vt_shape(bsz, n)],
        compiler_params=pltpu.CompilerParams(dimension_semantics=("parallel", "parallel"),
                                             vmem_limit_bytes=VMEM_LIMIT),
        name="diff_prep",
    )(p, g, cos, sin, s)


def _mla_prep_kernel(p_ref, qng_ref, kvng_ref, wuq_ref, wk_ref, wv_ref, gq_ref, gk_ref, gkr_ref, place_ref, s_ref,
                     cosa_ref, sina_ref, cosb_ref, sinb_ref, q_ref, k_ref, vt_ref):
    s = s_ref[...]
    half = MLA_ROPE // 2

    def rms(x, width):
        return x * lax.rsqrt(jnp.sum(x * x, axis=-1, keepdims=True) * (1.0 / width) + 1e-6)

    cqn = _bf(rms(p_ref[0, :, 0:MLA_Q_RANK], MLA_Q_RANK) * qng_ref[...])
    q = _dot(cqn, wuq_ref[...])
    cosa = jnp.concatenate([cosa_ref[...]] * MLA_HEADS, axis=1)
    sina = jnp.concatenate([sina_ref[...]] * MLA_HEADS, axis=1)
    qn = q * lax.rsqrt(_group_stat(q * q, s) + 1e-6) * gq_ref[...]
    qr = (qn * cosa + _swap_halves(qn, half) * sina) * ((MLA_NOPE + MLA_ROPE) ** -0.5 * LOG2E)
    q_ref[0] = _bf(qr)

    ckvn = _bf(rms(p_ref[0, :, MLA_Q_RANK:MLA_Q_RANK + MLA_KV_RANK], MLA_KV_RANK) * kvng_ref[...])
    kn = _dot(ckvn, wk_ref[...])
    knn = kn * lax.rsqrt(_group_stat(kn * kn, s) + 1e-6) * gk_ref[...]
    vt_ref[0, 0] = _bf(_dot(ckvn, wv_ref[...]).T)

    krn = rms(p_ref[0, :, 512:640], MLA_ROPE) * gkr_ref[...]
    krr = krn * cosb_ref[...] + _swap_halves(krn, half) * sinb_ref[...]
    k_ref[0] = _bf(knn + _dot(_bf(krr), place_ref[...]))


def _mla_prep(p, consts, tabs):
    bsz, n, _ = p.shape
    tm = ROW_TILE
    row = lambda b, i: (b, i, 0)
    const = lambda b, i: (0, 0)
    tab = pl.BlockSpec((tm, LANES), lambda b, i: (i, 0))
    qk_w = MLA_HEADS * LANES
    return pl.pallas_call(
        _mla_prep_kernel,
        grid=(bsz, n // tm),
        in_specs=[pl.BlockSpec((1, tm, P_MLA), row)] + [pl.BlockSpec(c.shape, const) for c in consts] + [tab] * 4,
        out_specs=[pl.BlockSpec((1, tm, qk_w), row), pl.BlockSpec((1, tm, qk_w), row), _VT_SPEC],
        out_shape=[jax.ShapeDtypeStruct((bsz, n, qk_w), BF16), jax.ShapeDtypeStruct((bsz, n, qk_w), BF16),
                   _vt_shape(bsz, n)],
        compiler_params=pltpu.CompilerParams(dimension_semantics=("parallel", "parallel"),
                                             vmem_limit_bytes=VMEM_LIMIT),
        name="mla_prep",
    )(p, *consts, *tabs)


def _attn_kernel(*refs, mode, n_keys, first_is_ctx, lam_init):
    if mode == "diff":
        q_ref, k_ref, vt_ref, lam_ref, g_ref, o_ref, s_sc = refs
    else:
        q_ref, k_ref, vt_ref, o_ref, s_sc = refs
    tq = q_ref.shape[1]
    qt = _bf(q_ref[0].astype(F32).T)
    row_lo = lax.broadcasted_iota(jnp.int32, (LANES, 1), 0) < HEAD_DIM
    if mode == "diff":
        zero = jnp.zeros_like(qt)
        qa, qb = jnp.where(row_lo, qt, zero), jnp.where(row_lo, zero, qt)
        v_dim, v_rows = LANES, (slice(None), slice(None))
    else:
        qa, qb = qt[:LANES], qt[LANES:]
        v_dim, v_rows = HEAD_DIM, (slice(0, HEAD_DIM), slice(HEAD_DIM, LANES))

    def sweep(n_iter, nb):
        def scores(c, slot):
            out = []
            for half, qm in enumerate((qa, qb)):
                mx = None
                for j in range(nb):
                    kj = k_ref[0, pl.ds(pl.multiple_of((c * nb + j) * ATTN_KB, ATTN_KB), ATTN_KB), :]
                    if mode != "diff":
                        kj = kj[:, half * LANES:(half + 1) * LANES]
                    sj = _dot(kj, qm)
                    s_sc[slot, half, j * ATTN_KB:(j + 1) * ATTN_KB, :] = sj
                    mj = jnp.max(sj, axis=0, keepdims=True)
                    mx = mj if mx is None else jnp.maximum(mx, mj)
                out.append(mx)
            return tuple(out)

        def accumulate(c, slot, mxs, state):
            new = []
            for half, (mx, (m, l, acc)) in enumerate(zip(mxs, state)):
                m_new = jnp.maximum(m, mx)
                alpha = jnp.exp2(m - m_new)
                l = alpha * l
                acc = alpha * acc
                for j in range(nb):
                    p = jnp.exp2(s_sc[slot, half, j * ATTN_KB:(j + 1) * ATTN_KB, :] - m_new)
                    l = l + jnp.sum(p, axis=0, keepdims=True)
                    acc = acc + _dot(vt_ref[0, c * nb + j, v_rows[half], :], _bf(p))
                new.append((m_new, l, acc))
            return tuple(new)

        def body(c2, carry):
            mxs, state = carry
            c = 2 * c2
            mxs1 = scores(c + 1, 1)
            state = accumulate(c, 0, mxs, state)
            mxs0 = scores(c + 2, 0)
            state = accumulate(c + 1, 1, mxs1, state)
            return mxs0, state

        init = (jnp.full((1, tq), -1e30, F32), jnp.zeros((1, tq), F32), jnp.zeros((v_dim, tq), F32))
        mxs, state = lax.fori_loop(0, (n_iter - 1) // 2, body, (scores(0, 0), (init, init)))
        (ma, la, acca), (mb, lb, accb) = accumulate(n_iter - 1, 0, mxs, state)
        return ma, la, acca, mb, lb, accb

    def finish(carry):
        _, la, acca, _, lb, accb = carry
        oa, ob = acca / la, accb / lb
        if mode == "diff":
            lam = lam_ref[...]
            lam_full = (jnp.exp(jnp.sum(lam[0:1] * lam[1:2], axis=-1, keepdims=True))
                        - jnp.exp(jnp.sum(lam[2:3] * lam[3:4], axis=-1, keepdims=True)) + lam_init)
            o = oa - lam_full * ob
            o = o * lax.rsqrt(jnp.mean(o * o, axis=0, keepdims=True) + 1e-6) * g_ref[...] * (1.0 - lam_init)
        else:
            o = jnp.concatenate([oa, ob], axis=0)
        o_ref[0] = _bf(o.T)

    n_blocks = n_keys // ATTN_KB
    if first_is_ctx:
        i = pl.program_id(2)

        @pl.when(i == 0)
        def _():
            finish(sweep(1, CTX_LEN // ATTN_KB))

        @pl.when(i > 0)
        def _():
            finish(sweep(n_blocks // ATTN_NB, ATTN_NB))
    else:
        finish(sweep(n_blocks // ATTN_NB, ATTN_NB))


def _attention(q, k, vt, extra, *, mode, want_ctx, lam_init=0.0):
    bsz, n, _ = k.shape
    tq = ROW_TILE
    off = 0 if want_ctx else 1
    n_out = n - off * tq
    qk_w = LANES if mode == "diff" else 2 * LANES
    in_specs = [pl.BlockSpec((1, tq, qk_w), lambda b, h, i: (b, i + off, h)),
                pl.BlockSpec((1, n, qk_w), lambda b, h, i: (b, 0, h)),
                pl.BlockSpec((1, n // ATTN_KB, LANES, ATTN_KB), lambda b, h, i: (b, 0, h, 0))]
    in_specs += [pl.BlockSpec(e.shape, lambda b, h, i: (0, 0)) for e in extra]
    return pl.pallas_call(
        functools.partial(_attn_kernel, mode=mode, n_keys=n, first_is_ctx=want_ctx, lam_init=lam_init),
        grid=(bsz, GROUP_W // LANES, n_out // tq),
        in_specs=in_specs,
        out_specs=pl.BlockSpec((1, tq, LANES), lambda b, h, i: (b, i, h)),
        out_shape=jax.ShapeDtypeStruct((bsz, n_out, GROUP_W), BF16),
        scratch_shapes=[pltpu.VMEM((2, 2, ATTN_NB * ATTN_KB, tq), F32)],
        compiler_params=pltpu.CompilerParams(dimension_semantics=("parallel", "parallel", "arbitrary"),
                                             vmem_limit_bytes=VMEM_LIMIT),
        name="attn_" + mode,
    )(q, k, vt, *extra)


def _halo_valid(i, n_tiles):
    return i >= 2, (i >= 1) & (i < n_tiles - 1)


def _rwkv_prep_kernel(p_ref, prev_ref, next_ref, sw_ref, w0_ref, wup_ref, a0_ref, aup_ref, gup_ref, kkw_ref, ka_ref,
                      ssum_ref, r_ref, k_ref, v_ref, kk_ref, lw_ref, a_ref, kd_ref, g_ref):
    i = pl.program_id(1)
    has_prev, has_next = _halo_valid(i, pl.num_programs(1))
    x = p_ref[0]
    prev = jnp.where(has_prev, prev_ref[0], 0.0)
    nxt = jnp.where(has_next, next_ref[0], 0.0)
    xm1, x0, xp1 = _shifted_rows(x, prev, nxt, (1, 0, -1))
    p = sw_ref[0:1, :] * xm1 + sw_ref[1:2, :] * x0 + sw_ref[2:3, :] * xp1

    r, k, v = p[:, 0:512], p[:, 512:1024], p[:, 1024:1536]
    w_raw = w0_ref[...] + _dot(_bf(jnp.tanh(p[:, 1536:1664])), wup_ref[...])
    lw = -RWKV_DECAY_SCALE * jax.nn.sigmoid(w_raw)
    a = jax.nn.sigmoid(a0_ref[...] + _dot(_bf(p[:, 1664:1792]), aup_ref[...]))
    kk = k * kkw_ref[...]
    kk = kk / jnp.maximum(jnp.sqrt(_group_stat(kk * kk, ssum_ref[...])), 1e-12)
    k2 = jnp.concatenate([k, k], axis=1)
    ka2 = jnp.concatenate([ka_ref[...]] * 2, axis=1)
    r_ref[0] = r
    k_ref[0] = k
    v_ref[0] = v
    kk_ref[0] = kk
    lw_ref[0] = lw
    a_ref[0] = a
    kd_ref[0] = k2 * (1.0 + (a - 1.0) * ka2)
    g_ref[0] = _dot(_bf(jax.nn.sigmoid(p[:, 1792:1920])), gup_ref[...])


def _halo_specs(width, tm):
    per = tm // HALO
    return [pl.BlockSpec((1, tm, width), lambda b, i: (b, i, 0)),
            pl.BlockSpec((1, HALO, width), lambda b, i: (b, jnp.maximum(i * per - 1, 0), 0)),
            pl.BlockSpec((1, HALO, width), lambda b, i: (b, jnp.minimum((i + 1) * per, pl.num_programs(1) * per - 1), 0))]


def _rwkv_prep(p, consts):
    bsz, n, _ = p.shape
    tm = ROW_TILE
    n_t = n // tm
    per = tm // HALO
    row = lambda b, i: (b, i, 0)
    const = lambda b, i: (0, 0)
    in_specs = [pl.BlockSpec((1, tm, P_RWKV), row),
                pl.BlockSpec((1, HALO, P_RWKV), lambda b, i: (b, jnp.maximum(i * per - 1, 0), 0)),
                pl.BlockSpec((1, HALO, P_RWKV), lambda b, i: (b, jnp.minimum((i + 1) * per, n_t * per - 1), 0))]
    in_specs += [pl.BlockSpec(c.shape, const) for c in consts]
    widths = (512, 512, 512, 512, 1024, 1024, 1024, 512)
    return pl.pallas_call(
        _rwkv_prep_kernel,
        grid=(bsz, n_t),
        in_specs=in_specs,
        out_specs=[pl.BlockSpec((1, tm, w), row) for w in widths],
        out_shape=[jax.ShapeDtypeStruct((bsz, n, w), F32) for w in widths],
        compiler_params=pltpu.CompilerParams(dimension_semantics=("parallel", "parallel"),
                                             vmem_limit_bytes=VMEM_LIMIT),
        name="rwkv_prep",
    )(p, p, p, *consts)


def _rwkv_scan_kernel(r_ref, kk_ref, v_ref, lw_ref, a_ref, kd_ref, y_ref, gt_sc):
    c = RWKV_CHUNK
    fwd = pl.program_id(0) == 0

    @pl.when(pl.program_id(2) == 0)
    def _():
        gt_sc[...] = jnp.zeros_like(gt_sc)

    ti = lax.broadcasted_iota(jnp.int32, (c, c), 0)
    si = lax.broadcasted_iota(jnp.int32, (c, c), 1)
    tri = jnp.where(jnp.where(fwd, ti - si, si - ti) >= 0, 1.0, 0.0).astype(BF16)
    t2 = lax.broadcasted_iota(jnp.int32, (2 * c, 2 * c), 0)
    s2 = lax.broadcasted_iota(jnp.int32, (2 * c, 2 * c), 1)
    lag = jnp.where(fwd, t2 - s2, s2 - t2)
    lag = jnp.where((t2 & c) == (s2 & c), lag, -1)
    strict = lag > 0
    incl = lag >= 0
    eye = jnp.where(t2 == s2, 1.0, 0.0)
    m0 = lax.broadcasted_iota(jnp.int32, (1, LANES), 1) < HEAD_DIM

    def stack(x):
        return jnp.concatenate([jnp.where(m0, x, 0.0), jnp.where(m0, 0.0, x)], axis=0)

    pairs = range(GROUP_W // LANES)
    sls = [slice(p * LANES, (p + 1) * LANES) for p in pairs]
    la_b, lr_b, rb_b, rk_b, vs, vs_b, rr_b, p_end = [], [], [], [], [], [], [], []
    for sl in sls:
        lw = lw_ref[0, :, sl]
        cum = _dot_exact_lhs(tri, lw)
        e_in, e_ex, e_neg = jnp.exp(cum), jnp.exp(cum - lw), jnp.exp(-cum)
        pe = jnp.exp(jnp.where(fwd, cum[c - 1:c, :], cum[0:1, :]))
        kk = kk_ref[0, :, sl]
        rb = stack(-(kk * a_ref[0, :, sl]) * e_neg)
        rk = stack(kd_ref[0, :, sl] * e_neg)
        la_b.append(_bf(stack(kk * e_ex)))
        lr_b.append(_bf(stack(r_ref[0, :, sl] * e_in)))
        rb_b.append(_bf(rb))
        rk_b.append(_bf(rk))
        vs.append(stack(v_ref[0, :, sl]))
        vs_b.append(_bf(vs[-1]))
        rr_b.append(_bf(jnp.concatenate([rb * pe, rk * pe], axis=0)))
        p_end.append(pe)

    a_ab = [jnp.where(strict, _dot_nt(x, y), 0.0) for x, y in zip(la_b, rb_b)]
    a_ak = [_bf(jnp.where(strict, _dot_nt(x, y), 0.0)) for x, y in zip(la_b, rk_b)]
    a_rb = [_bf(jnp.where(incl, _dot_nt(x, y), 0.0)) for x, y in zip(lr_b, rb_b)]
    a_rk = [_bf(jnp.where(incl, _dot_nt(x, y), 0.0)) for x, y in zip(lr_b, rk_b)]

    tm = [eye + x for x in a_ab]
    apow = a_ab
    for _ in range(5):
        apow = [_dot(_bf(x), _bf(x)) for x in apow]
        tm = [t + _dot(_bf(t), _bf(x)) for t, x in zip(tm, apow)]
    tm_b = [_bf(t) for t in tm]
    akv = [_bf(_dot(x, y)) for x, y in zip(a_ak, vs_b)]
    w = [_bf(_dot(t, x)) for t, x in zip(tm_b, la_b)]
    u0 = [_dot(t, x) for t, x in zip(tm_b, akv)]
    y0 = [_dot(x, y) for x, y in zip(a_rk, vs_b)]

    gt = [gt_sc[p] for p in pairs]
    gt_b = [_bf(g) for g in gt]
    u = [_dot_nt(x, g) + z for x, g, z in zip(w, gt_b, u0)]
    y = [_dot_nt(x, g) + _dot(ar, _bf(uu)) + z for x, g, ar, uu, z in zip(lr_b, gt_b, a_rb, u, y0)]
    for p in pairs:
        y_ref[0, 0, :, sls[p]] = y[p][:c] + y[p][c:]
        uv = jnp.concatenate([u[p], vs[p]], axis=0)
        gt_sc[p] = gt[p] * p_end[p] + _dot(_bf(uv.T), rr_b[p])


def _scan_chunk_index(d, s, n_ctx, n_all):
    return jnp.where(d == 0, s, jnp.where(s < n_ctx, n_ctx - 1 - s, n_all + n_ctx - 1 - s))


def _rwkv_scan(r, kk, v, lw, a, kd):
    bsz, n, _ = r.shape
    c = RWKV_CHUNK
    n_all, n_ctx = n // c, CTX_LEN // c
    shared = pl.BlockSpec((1, c, GROUP_W), lambda d, b, s: (b, _scan_chunk_index(d, s, n_ctx, n_all), 0))
    per_dir = pl.BlockSpec((1, c, GROUP_W), lambda d, b, s: (b, _scan_chunk_index(d, s, n_ctx, n_all), d))
    return pl.pallas_call(
        _rwkv_scan_kernel,
        grid=(2, bsz, n_all),
        in_specs=[shared, shared, shared, per_dir, per_dir, per_dir],
        out_specs=pl.BlockSpec((1, 1, c, GROUP_W), lambda d, b, s: (d, b, _scan_chunk_index(d, s, n_ctx, n_all), 0)),
        out_shape=jax.ShapeDtypeStruct((2, bsz, n, GROUP_W), F32),
        scratch_shapes=[pltpu.VMEM((GROUP_W // LANES, LANES, LANES), F32)],
        compiler_params=pltpu.CompilerParams(dimension_semantics=("parallel", "parallel", "arbitrary"),
                                             vmem_limit_bytes=VMEM_LIMIT),
        name="rwkv_scan",
    )(r, kk, v, lw, a, kd)


def _ssd_prep_kernel(p_ref, prev_ref, next_ref, cw_ref, cb_ref, dtb_ref, aneg_ref, e_ref, pa_ref,
                     xs_ref, bc_ref, xdt_ref, adt_ref):
    i = pl.program_id(1)
    has_prev, has_next = _halo_valid(i, pl.num_programs(1))
    x = p_ref[0, :, 512:1536]
    prev = jnp.where(has_prev, prev_ref[0, :, 512:1536], 0.0)
    nxt = jnp.where(has_next, next_ref[0, :, 512:1536], 0.0)
    taps = _shifted_rows(x, prev, nxt, (2, 1, 0, -1, -2))
    conv = cb_ref[...]
    for j, tap in enumerate(taps):
        conv = conv + cw_ref[j:j + 1, :] * tap
    xbc = _silu(conv)
    xs = xbc[:, 0:512]
    xs_ref[0] = xs
    bc_ref[0] = xbc[:, 512:1024]
    t = p_ref[0, :, 1536:1664] + dtb_ref[...]
    dt = jnp.maximum(t, 0.0) + jnp.log(1.0 + jnp.exp(-jnp.abs(t)))
    xdt_ref[0] = jnp.concatenate([xs, xs], axis=1) * _dot_exact_rhs(dt, e_ref[...])
    adt_ref[0] = _dot_exact_rhs(dt * aneg_ref[...], pa_ref[...])


def _ssd_prep(p, consts):
    bsz, n, _ = p.shape
    tm = ROW_TILE
    n_t = n // tm
    per = tm // HALO
    row = lambda b, i: (b, i, 0)
    const = lambda b, i: (0, 0)
    in_specs = [pl.BlockSpec((1, tm, P_SSD), row),
                pl.BlockSpec((1, HALO, P_SSD), lambda b, i: (b, jnp.maximum(i * per - 1, 0), 0)),
                pl.BlockSpec((1, HALO, P_SSD), lambda b, i: (b, jnp.minimum((i + 1) * per, n_t * per - 1), 0))]
    in_specs += [pl.BlockSpec(c.shape, const) for c in consts]
    widths = (512, 512, 1024, 256)
    return pl.pallas_call(
        _ssd_prep_kernel,
        grid=(bsz, n_t),
        in_specs=in_specs,
        out_specs=[pl.BlockSpec((1, tm, w), row) for w in widths],
        out_shape=[jax.ShapeDtypeStruct((bsz, n, w), F32) for w in widths],
        compiler_params=pltpu.CompilerParams(dimension_semantics=("parallel", "parallel"),
                                             vmem_limit_bytes=VMEM_LIMIT),
        name="ssd_prep",
    )(p, p, p, *consts)


def _ssd_scan_kernel(x_ref, a_ref, bc_ref, y_ref, ht_sc):
    q = SSD_CHUNK
    fwd = pl.program_id(0) == 0

    @pl.when(pl.program_id(2) == 0)
    def _():
        ht_sc[...] = jnp.zeros_like(ht_sc)

    ti = lax.broadcasted_iota(jnp.int32, (q, q), 0)
    si = lax.broadcasted_iota(jnp.int32, (q, q), 1)
    tri = jnp.where(fwd, ti - si, si - ti) >= 0
    acum = _dot_exact_lhs(jnp.where(tri, 1.0, 0.0).astype(BF16), a_ref[0])
    acum_t = acum.T
    last = jnp.where(fwd, acum[q - 1:q, :], acum[0:1, :])
    m0 = lax.broadcasted_iota(jnp.int32, (1, LANES), 1) < HEAD_DIM
    bc = bc_ref[0]

    def decay(h):
        seg = acum[:, h:h + 1] - acum_t[h:h + 1, :]
        return jnp.where(tri, jnp.exp(jnp.where(tri, seg, 0.0)), 0.0)

    def per_head(fn, h0):
        return jnp.where(m0, fn(h0), fn(h0 + 1))

    for g in range(2):
        bg = bc[:, g * SSD_STATE:(g + 1) * SSD_STATE]
        cg = bc[:, 256 + g * SSD_STATE:256 + (g + 1) * SSD_STATE]
        cg_b = _bf(cg)
        scores = _dot_nt(cg_b, _bf(bg))
        bg_t = _bf(bg.T)
        for pp in range(2):
            pair = 2 * g + pp
            h0 = 2 * pair
            sl = slice(pair * LANES, (pair + 1) * LANES)
            xp = x_ref[0, :, sl]
            xp_b = _bf(xp)
            y0 = _dot(_bf(scores * decay(h0)), xp_b)
            y1 = _dot(_bf(scores * decay(h0 + 1)), xp_b)
            ht = ht_sc[pair]
            carry = _dot(cg_b, _bf(ht)) * per_head(lambda h: jnp.exp(acum[:, h:h + 1]), h0)
            y_ref[0, 0, :, sl] = jnp.where(m0, y0, y1) + carry
            to_end = per_head(lambda h: jnp.exp(last[:, h:h + 1] - acum[:, h:h + 1]), h0)
            e_last = per_head(lambda h: jnp.exp(last[:, h:h + 1]), h0)
            ht_sc[pair] = ht * e_last + _dot(bg_t, _bf(xp * to_end))


def _ssd_scan(xdt, adt, bc):
    bsz, n, _ = bc.shape
    q = SSD_CHUNK
    n_all, n_ctx = n // q, CTX_LEN // q
    idx = lambda d, s: _scan_chunk_index(d, s, n_ctx, n_all)
    return pl.pallas_call(
        _ssd_scan_kernel,
        grid=(2, bsz, n_all),
        in_specs=[pl.BlockSpec((1, q, GROUP_W), lambda d, b, s: (b, idx(d, s), d)),
                  pl.BlockSpec((1, q, LANES), lambda d, b, s: (b, idx(d, s), d)),
                  pl.BlockSpec((1, q, GROUP_W), lambda d, b, s: (b, idx(d, s), 0))],
        out_specs=pl.BlockSpec((1, 1, q, GROUP_W), lambda d, b, s: (d, b, idx(d, s), 0)),
        out_shape=jax.ShapeDtypeStruct((2, bsz, n, GROUP_W), F32),
        scratch_shapes=[pltpu.VMEM((GROUP_W // LANES, SSD_STATE, LANES), F32)],
        compiler_params=pltpu.CompilerParams(dimension_semantics=("parallel", "parallel", "arbitrary"),
                                             vmem_limit_bytes=VMEM_LIMIT),
        name="ssd_scan",
    )(xdt, adt, bc)


def _post_kernel(z_ref, mctx_ref, mlat_ref, od_ref, om_ref, yrf_ref, yrb_ref, r_ref, k_ref, v_ref, g_ref,
                 lng_ref, lnb_ref, rk_ref, ysf_ref, ysb_ref, xs_ref, zg_ref, dsk_ref, ng_ref, smean_ref, ssum_ref,
                 w_ref, o_ref, *, tile_off):
    is_ctx = _is_ctx_rows(pl.program_id(1) + tile_off, ROW_TILE, CTX_LEN)
    smean = smean_ref[...]
    y = yrf_ref[0, 0] + yrb_ref[0, 0]
    yc = y - _group_stat(y, smean)
    yn = yc * lax.rsqrt(_group_stat(yc * yc, smean) + RWKV_LN_EPS) * lng_ref[...] + lnb_ref[...]
    bonus = _group_stat(r_ref[0] * k_ref[0] * rk_ref[...], ssum_ref[...]) * v_ref[0]
    o_rwkv = (yn + bonus) * g_ref[0]

    t = (ysf_ref[0, 0] + ysb_ref[0, 0] + dsk_ref[...] * xs_ref[0]) * _silu(zg_ref[0])
    o_ssd = t * lax.rsqrt(jnp.mean(t * t, axis=-1, keepdims=True) + 1e-6) * ng_ref[...]

    mix = (_dot(od_ref[0], w_ref[0:512, :]) + _dot(_bf(o_rwkv), w_ref[512:1024, :])
           + _dot(om_ref[0], w_ref[1024:1536, :]) + _dot(_bf(o_ssd), w_ref[1536:2048, :]))
    o_ref[0] = z_ref[0] + _mod_row(mctx_ref, mlat_ref, 5, is_ctx) * mix


def _post(z, m, od, om, yr, r, k, v, g, ys, xs, p_ssd, consts, w, *, want_ctx):
    bsz, n, d = z.shape
    tm = ROW_TILE
    off = 0 if want_ctx else 1
    n_out = n - off * tm
    full = lambda b, i: (b, i + off, 0)
    attn = lambda b, i: (b, i, 0)
    const = lambda b, i: (0, 0)
    seq = lambda width: pl.BlockSpec((1, tm, width), full)
    dirs = lambda dd: pl.BlockSpec((1, 1, tm, GROUP_W), lambda b, i: (dd, b, i + off, 0))
    in_specs = [seq(d),
                pl.BlockSpec((1, N_MOD, d), lambda b, i: (0, 0, 0)),
                pl.BlockSpec((1, N_MOD, d), lambda b, i: (b + 1, 0, 0)),
                pl.BlockSpec((1, tm, GROUP_W), attn), pl.BlockSpec((1, tm, GROUP_W), attn),
                dirs(0), dirs(1), seq(GROUP_W), seq(GROUP_W), seq(GROUP_W), seq(GROUP_W)]
    in_specs += [pl.BlockSpec(c.shape, const) for c in consts[:3]]
    in_specs += [dirs(0), dirs(1), seq(GROUP_W), seq(GROUP_W)]
    in_specs += [pl.BlockSpec(c.shape, const) for c in consts[3:]]
    in_specs += [pl.BlockSpec(w.shape, const, pipeline_mode=pl.Buffered(1))]
    return pl.pallas_call(
        functools.partial(_post_kernel, tile_off=off),
        grid=(bsz, n_out // tm),
        in_specs=in_specs,
        out_specs=pl.BlockSpec((1, tm, d), lambda b, i: (b, i, 0)),
        out_shape=jax.ShapeDtypeStruct((bsz, n_out, d), F32),
        compiler_params=pltpu.CompilerParams(dimension_semantics=("parallel", "parallel"),
                                             vmem_limit_bytes=VMEM_LIMIT),
        name="post",
    )(z, m, m, od, om, yr, yr, r, k, v, g, *consts[:3], ys, ys, xs, p_ssd, *consts[3:], w)


def _pad_cols(w, width):
    return jnp.pad(w, ((0, 0), (0, width - w.shape[1])))


def _block_diag_const(block, reps):
    return jnp.asarray(np.kron(np.eye(reps, dtype=np.float32), block), BF16)


def _group_consts():
    ones64 = np.ones((64, 64), np.float32)
    mla_blk = np.zeros((LANES, LANES), np.float32)
    mla_blk[:MLA_NOPE, :MLA_NOPE] = 1.0 / MLA_NOPE
    mla_blk[MLA_NOPE:MLA_NOPE + MLA_ROPE, MLA_NOPE:MLA_NOPE + MLA_ROPE] = 1.0 / MLA_ROPE
    return (_block_diag_const(ones64 / 64.0, 4), _block_diag_const(ones64, 4), _block_diag_const(mla_blk, 2))


def _rope_angles(n_lat, dim):
    quarter = dim // 4
    inv = ROPE_BASE ** (-jnp.arange(quarter, dtype=F32) / quarter)
    rows = n_lat // GRID_W
    pos_r = jnp.repeat(jnp.arange(rows), GRID_W).astype(F32)
    pos_c = jnp.tile(jnp.arange(GRID_W), rows).astype(F32)
    ang = jnp.concatenate([pos_r[:, None] * inv, pos_c[:, None] * inv], axis=-1)
    return jnp.cos(ang), jnp.sin(ang)


def _rope_lane_tables(n_lat, dim, lane_off):
    cos, sin = _rope_angles(n_lat, dim)
    cos_t = jnp.ones((n_lat, LANES), F32).at[:, lane_off:lane_off + dim].set(jnp.concatenate([cos, cos], axis=-1))
    sin_t = jnp.zeros((n_lat, LANES), F32).at[:, lane_off:lane_off + dim].set(jnp.concatenate([-sin, sin], axis=-1))
    cos_t = jnp.concatenate([jnp.ones((CTX_LEN, LANES), F32), cos_t], axis=0)
    sin_t = jnp.concatenate([jnp.zeros((CTX_LEN, LANES), F32), sin_t], axis=0)
    return cos_t, sin_t


def _row(x):
    return x.reshape(1, -1).astype(F32)


def _mla_consts(q_norm_g, kv_norm_g, w_uq, w_ukv, nope_g, rope_g, s_mla):
    hq = MLA_NOPE + MLA_ROPE
    w_uq = w_uq.reshape(MLA_Q_RANK, MLA_HEADS, hq)
    w_uq = jnp.pad(w_uq, ((0, 0), (0, 0), (0, LANES - hq))).reshape(MLA_Q_RANK, MLA_HEADS * LANES)
    w_ukv = w_ukv.reshape(MLA_KV_RANK, MLA_HEADS, 2 * MLA_NOPE)
    w_k = jnp.pad(w_ukv[:, :, :MLA_NOPE], ((0, 0), (0, 0), (0, LANES - MLA_NOPE))).reshape(MLA_KV_RANK, -1)
    w_v = w_ukv[:, :, MLA_NOPE:].reshape(MLA_KV_RANK, GROUP_W)
    zpad = jnp.zeros((LANES - hq,), F32)
    g_q = jnp.tile(jnp.concatenate([nope_g[0], rope_g[0], zpad]), MLA_HEADS)
    g_k = jnp.tile(jnp.concatenate([nope_g[1], jnp.zeros((LANES - MLA_NOPE,), F32)]), MLA_HEADS)
    g_kr = jnp.concatenate([rope_g[1], jnp.zeros((LANES - MLA_ROPE,), F32)])
    place = np.zeros((LANES, MLA_HEADS * LANES), np.float32)
    for h in range(MLA_HEADS):
        place[np.arange(MLA_ROPE), h * LANES + MLA_NOPE + np.arange(MLA_ROPE)] = 1.0
    return (_row(q_norm_g), _row(kv_norm_g), _bf(w_uq), _bf(w_k), _bf(w_v), _row(g_q), _row(g_k), _row(g_kr),
            jnp.asarray(place, BF16), s_mla)


def _two_dir_block_diag(w):
    z = jnp.zeros_like(w[0])
    return jnp.concatenate([jnp.concatenate([w[0], z], axis=1), jnp.concatenate([z, w[1]], axis=1)], axis=0)


def _ssd_expand_consts():
    e = np.zeros((LANES, 2 * GROUP_W), np.float32)
    pa = np.zeros((LANES, 2 * LANES), np.float32)
    for d in range(2):
        for h in range(SSD_HEADS):
            e[d * SSD_HEADS + h, d * GROUP_W + h * HEAD_DIM:d * GROUP_W + (h + 1) * HEAD_DIM] = 1.0
            pa[d * SSD_HEADS + h, d * LANES + h] = 1.0
    return jnp.asarray(e, BF16), jnp.asarray(pa, BF16)


def _pad_lanes(v):
    return jnp.pad(v.reshape(1, -1).astype(F32), ((0, 0), (0, LANES - v.size)))


def kernel(x, c, ctx, c_ctx, mod_w, mod_b, norm_g, ffn_w_gate, ffn_w_up, ffn_w_down, w_in, w_out, diff_qk_g, diff_lambda, diff_subln_g, rwkv_shift_w, rwkv_w0, rwkv_w_up, rwkv_a0, rwkv_a_up, rwkv_g_up, rwkv_k_k, rwkv_k_a, rwkv_r_k, rwkv_ln_g, rwkv_ln_b, mla_q_norm_g, mla_kv_norm_g, mla_w_uq, mla_w_ukv, mla_nope_g, mla_rope_g, ssd_conv_w, ssd_conv_b, ssd_dt_bias, ssd_a_log, ssd_d, ssd_norm_g):
    bsz, n_lat, d = x.shape
    depth = mod_w.shape[0]
    s_mean64, s_sum64, s_mla = _group_consts()
    e_dt, p_adt = _ssd_expand_consts()
    cos_d, sin_d = _rope_lane_tables(n_lat, HEAD_DIM, 0)
    cos_d = cos_d.at[:, HEAD_DIM:].set(cos_d[:, :HEAD_DIM])
    sin_d = sin_d.at[:, HEAD_DIM:].set(sin_d[:, :HEAD_DIM])
    cos_ma, sin_ma = _rope_lane_tables(n_lat, MLA_ROPE, MLA_NOPE)
    cos_mb, sin_mb = _rope_lane_tables(n_lat, MLA_ROPE, 0)

    cvec = jnp.zeros((8, d), F32).at[0].set(c_ctx).at[1:1 + bsz].set(c)
    z = jnp.concatenate([ctx, x], axis=1)
    tm_ffn = 640

    for l in range(depth):
        want_ctx = l < depth - 1
        lam_init = 0.8 - 0.6 * float(np.exp(-0.3 * l))
        m = _mod(cvec, mod_w[l], mod_b[l].reshape(1, N_MOD * d)).reshape(8, N_MOD, d)

        z = _ffn(z, m, _row(norm_g[l, 0]), _bf(ffn_w_gate[l, 0]), _bf(ffn_w_up[l, 0]), _bf(ffn_w_down[l, 0]),
                 j=0, ctx_len=CTX_LEN, tm=tm_ffn)

        w = w_in[l]
        w_p = jnp.concatenate([w[:, :3456], _pad_cols(w[:, 3456:4000], P_MLA), _pad_cols(w[:, 4000:], P_SSD)], axis=1)
        p_diff, p_rwkv, p_mla, p_ssd = _inproj(z, m, _row(norm_g[l, 1]), _bf(w_p))

        qd, kd, vd = _diff_prep(p_diff, jnp.tile(diff_qk_g[l], (1, 2 * DIFF_HEADS)), cos_d, sin_d, s_mean64)
        o_diff = _attention(qd, kd, vd, (diff_lambda[l], diff_subln_g[l].reshape(-1, 1)), mode="diff", want_ctx=want_ctx,
                            lam_init=lam_init)

        qm, km, vm = _mla_prep(p_mla, _mla_consts(mla_q_norm_g[l], mla_kv_norm_g[l], mla_w_uq[l], mla_w_ukv[l],
                                                  mla_nope_g[l], mla_rope_g[l], s_mla),
                               (cos_ma, sin_ma, cos_mb, sin_mb))
        o_mla = _attention(qm, km, vm, (), mode="mla", want_ctx=want_ctx)

        rw_consts = (rwkv_shift_w[l], _row(rwkv_w0[l]), _bf(_two_dir_block_diag(rwkv_w_up[l])), _row(rwkv_a0[l]),
                     _bf(_two_dir_block_diag(rwkv_a_up[l])), _bf(rwkv_g_up[l]), _row(rwkv_k_k[l]),
                     _row(rwkv_k_a[l]), s_sum64)
        r, k, v, kk, lw, a, kdir, g = _rwkv_prep(p_rwkv, rw_consts)
        y_rwkv = _rwkv_scan(r, kk, v, lw, a, kdir)

        sd_consts = (ssd_conv_w[l], _row(ssd_conv_b[l]), _pad_lanes(ssd_dt_bias[l]),
                     _pad_lanes(-jnp.exp(ssd_a_log[l])), e_dt, p_adt)
        xs, bc, xdt, adt = _ssd_prep(p_ssd, sd_consts)
        y_ssd = _ssd_scan(xdt, adt, bc)

        post_consts = (_row(rwkv_ln_g[l]), _row(rwkv_ln_b[l]), _row(rwkv_r_k[l]),
                       _row(jnp.repeat(ssd_d[l], HEAD_DIM)), _row(ssd_norm_g[l]), s_mean64, s_sum64)
        z = _post(z, m, o_diff, o_mla, y_rwkv, r, k, v, g, y_ssd, xs, p_ssd, post_consts, _bf(w_out[l]),
                  want_ctx=want_ctx)

        z = _ffn(z, m, _row(norm_g[l, 2]), _bf(ffn_w_gate[l, 1]), _bf(ffn_w_up[l, 1]), _bf(ffn_w_down[l, 1]),
                 j=2, ctx_len=CTX_LEN if want_ctx else 0, tm=tm_ffn if want_ctx else 512)
    return z
```

```python
import functools

import jax
import jax.numpy as jnp
import numpy as np
from jax import lax
from jax.experimental import pallas as pl
from jax.experimental.pallas import tpu as pltpu

F32 = jnp.float32
BF16 = jnp.bfloat16

CTX_LEN = 256
GRID_W = 64
HEAD_DIM = 64
ROPE_BASE = 10000.0
N_MOD = 9
GROUP_W = 512
DIFF_HEADS = 4
MLA_HEADS = 8
MLA_NOPE = 64
MLA_ROPE = 32
MLA_Q_RANK = 384
MLA_KV_RANK = 128
RWKV_DECAY_SCALE = 0.606531
RWKV_LN_EPS = 64e-5
RWKV_CHUNK = 64
SSD_HEADS = 8
SSD_STATE = 128
SSD_CHUNK = 128
LOG2E = 1.4426950408889634

LANES = 128
ROW_TILE = CTX_LEN
HALO = 8
ATTN_KB = ROW_TILE
ATTN_NB = 5
ONES_ROWS = 16
VMEM_LIMIT = 56 * 1024 * 1024

P_DIFF = 1536
P_RWKV = 1920
P_MLA = 640
P_SSD = 1664
P_OFFS = (0, P_DIFF, P_DIFF + P_RWKV, P_DIFF + P_RWKV + P_MLA, P_DIFF + P_RWKV + P_MLA + P_SSD)


def _dot(a, b):
    return jnp.dot(a, b, preferred_element_type=F32)


def _dot_nt(a, b):
    return lax.dot_general(a, b, (((1,), (1,)), ((), ())), preferred_element_type=F32)


def _bf(x):
    return x.astype(BF16)


def _split3(x):
    x1 = x.astype(BF16)
    r1 = x - x1.astype(F32)
    x2 = r1.astype(BF16)
    r2 = r1 - x2.astype(F32)
    return x1, x2, r2.astype(BF16)


def _dot_exact_lhs(a, x):
    x1, x2, x3 = _split3(x)
    return _dot(a, x1) + _dot(a, x2) + _dot(a, x3)


def _dot_exact_rhs(x, a):
    x1, x2, x3 = _split3(x)
    return _dot(x1, a) + _dot(x2, a) + _dot(x3, a)


def _group_stat(x, s):
    hi = x.astype(BF16)
    lo = (x - hi.astype(F32)).astype(BF16)
    blk = s.shape[0]
    outs = []
    for c in range(x.shape[-1] // blk):
        sl = slice(c * blk, (c + 1) * blk)
        outs.append(_dot(hi[:, sl], s) + _dot(lo[:, sl], s))
    return outs[0] if len(outs) == 1 else jnp.concatenate(outs, axis=-1)


def _swap_halves(x, half):
    n = x.shape[-1]
    lane = lax.broadcasted_iota(jnp.int32, (1, n), 1)
    up = pltpu.roll(x, n - half, axis=1)
    dn = pltpu.roll(x, half, axis=1)
    return jnp.where((lane & (2 * half - 1)) < half, up, dn)


def _silu(x):
    return x * jax.nn.sigmoid(x)


def _is_ctx_rows(tile_idx, tm, ctx_len):
    rows = tile_idx * tm + lax.broadcasted_iota(jnp.int32, (tm, 1), 0)
    return rows < ctx_len


def _mod_row(mctx_ref, mlat_ref, idx, is_ctx):
    return jnp.where(is_ctx, mctx_ref[0, idx:idx + 1, :], mlat_ref[0, idx:idx + 1, :])


def _adaln(z, g, mctx_ref, mlat_ref, j, is_ctx):
    ms = jnp.mean(z * z, axis=-1, keepdims=True)
    h = z * lax.rsqrt(ms + 1e-6) * g
    return h * (1.0 + _mod_row(mctx_ref, mlat_ref, 3 * j + 1, is_ctx)) + _mod_row(mctx_ref, mlat_ref, 3 * j, is_ctx)


def _shifted_rows(x, prev, nxt, shifts):
    tm = x.shape[0]
    ext = jnp.concatenate([prev, x, nxt], axis=0)
    n = tm + 2 * HALO
    out = []
    for s in shifts:
        out.append(x if s == 0 else pltpu.roll(ext, s % n, axis=0)[HALO:HALO + tm])
    return out


def _mod_kernel(c_ref, w_ref, b_ref, o_ref):
    o_ref[...] = _dot(_bf(_silu(c_ref[...])), _bf(w_ref[...])) + b_ref[...]


def _mod(cvec, w, b):
    d, n = w.shape
    tn = 1024
    return pl.pallas_call(
        _mod_kernel,
        grid=(n // tn,),
        in_specs=[pl.BlockSpec((8, d), lambda i: (0, 0)),
                  pl.BlockSpec((d, tn), lambda i: (0, i)),
                  pl.BlockSpec((1, tn), lambda i: (0, i))],
        out_specs=pl.BlockSpec((8, tn), lambda i: (0, i)),
        out_shape=jax.ShapeDtypeStruct((8, n), F32),
        compiler_params=pltpu.CompilerParams(dimension_semantics=("parallel",), vmem_limit_bytes=VMEM_LIMIT),
        name="mod",
    )(cvec, w, b)


def _ffn_kernel(z_ref, mctx_ref, mlat_ref, g_ref, wg_ref, wu_ref, wd_ref, o_ref, h_sc, acc_sc, *, j, ctx_len, tm):
    f = pl.program_id(2)
    is_ctx = _is_ctx_rows(pl.program_id(1), tm, ctx_len)

    @pl.when(f == 0)
    def _():
        h_sc[...] = _bf(_adaln(z_ref[0], g_ref[...], mctx_ref, mlat_ref, j, is_ctx))
        acc_sc[...] = jnp.zeros_like(acc_sc)

    h = h_sc[...]
    act = _silu(_dot(h, wg_ref[...])) * _dot(h, wu_ref[...])
    acc_sc[...] += _dot(_bf(act), wd_ref[...])

    @pl.when(f == pl.num_programs(2) - 1)
    def _():
        gate = _mod_row(mctx_ref, mlat_ref, 3 * j + 2, is_ctx)
        o_ref[0] = z_ref[0] + gate * (0.5 * acc_sc[...])


def _ffn(z, m, g, wg, wu, wd, *, j, ctx_len, tm):
    bsz, n, d = z.shape
    ff = wg.shape[1]
    tf = 512
    return pl.pallas_call(
        functools.partial(_ffn_kernel, j=j, ctx_len=ctx_len, tm=tm),
        grid=(bsz, n // tm, ff // tf),
        in_specs=[pl.BlockSpec((1, tm, d), lambda b, i, f: (b, i, 0)),
                  pl.BlockSpec((1, N_MOD, d), lambda b, i, f: (0, 0, 0)),
                  pl.BlockSpec((1, N_MOD, d), lambda b, i, f: (b + 1, 0, 0)),
                  pl.BlockSpec((1, d), lambda b, i, f: (0, 0)),
                  pl.BlockSpec((d, tf), lambda b, i, f: (0, f)),
                  pl.BlockSpec((d, tf), lambda b, i, f: (0, f)),
                  pl.BlockSpec((tf, d), lambda b, i, f: (f, 0))],
        out_specs=pl.BlockSpec((1, tm, d), lambda b, i, f: (b, i, 0)),
        out_shape=jax.ShapeDtypeStruct(z.shape, F32),
        scratch_shapes=[pltpu.VMEM((tm, d), BF16), pltpu.VMEM((tm, d), F32)],
        compiler_params=pltpu.CompilerParams(dimension_semantics=("parallel", "parallel", "arbitrary"),
                                             vmem_limit_bytes=VMEM_LIMIT),
        name="ffn",
    )(z, m, m, g, wg, wu, wd)


def _inproj_kernel(z_ref, mctx_ref, mlat_ref, g_ref, w_ref, od_ref, or_ref, om_ref, os_ref):
    is_ctx = _is_ctx_rows(pl.program_id(1), ROW_TILE, CTX_LEN)
    h = _bf(_adaln(z_ref[0], g_ref[...], mctx_ref, mlat_ref, 1, is_ctx))
    for o_ref, lo, hi in zip((od_ref, or_ref, om_ref, os_ref), P_OFFS[:-1], P_OFFS[1:]):
        o_ref[0] = _dot(h, w_ref[:, lo:hi])


def _inproj(z, m, g, w):
    bsz, n, d = z.shape
    tm = ROW_TILE
    widths = (P_DIFF, P_RWKV, P_MLA, P_SSD)
    return pl.pallas_call(
        _inproj_kernel,
        grid=(bsz, n // tm),
        in_specs=[pl.BlockSpec((1, tm, d), lambda b, i: (b, i, 0)),
                  pl.BlockSpec((1, N_MOD, d), lambda b, i: (0, 0, 0)),
                  pl.BlockSpec((1, N_MOD, d), lambda b, i: (b + 1, 0, 0)),
                  pl.BlockSpec((1, d), lambda b, i: (0, 0)),
                  pl.BlockSpec(w.shape, lambda b, i: (0, 0), pipeline_mode=pl.Buffered(1))],
        out_specs=[pl.BlockSpec((1, tm, wd), lambda b, i: (b, i, 0)) for wd in widths],
        out_shape=[jax.ShapeDtypeStruct((bsz, n, wd), F32) for wd in widths],
        compiler_params=pltpu.CompilerParams(dimension_semantics=("parallel", "parallel"),
                                             vmem_limit_bytes=VMEM_LIMIT),
        name="inproj",
    )(z, m, m, g, w)


_VT_SPEC = pl.BlockSpec((1, 1, GROUP_W, ROW_TILE), lambda b, i: (b, i, 0, 0))


def _vt_shape(bsz, n):
    return jax.ShapeDtypeStruct((bsz, n // ROW_TILE, GROUP_W, ROW_TILE), BF16)


def _diff_prep_kernel(p_ref, g_ref, cos_ref, sin_ref, s_ref, q_ref, k_ref, vt_ref):
    s = s_ref[...]
    cos = jnp.concatenate([cos_ref[...]] * 4, axis=1)
    sin = jnp.concatenate([sin_ref[...]] * 4, axis=1)

    def norm_rope(x, g):
        xn = x * lax.rsqrt(_group_stat(x * x, s) + 1e-6) * g
        return xn * cos + _swap_halves(xn, HEAD_DIM // 2) * sin

    q = norm_rope(p_ref[0, :, 0:512], g_ref[0:1, :]) * (HEAD_DIM ** -0.5 * LOG2E)
    k = norm_rope(p_ref[0, :, 512:1024], g_ref[1:2, :])
    q_ref[0] = _bf(q)
    k_ref[0] = _bf(k)
    vt_ref[0, 0] = _bf(p_ref[0, :, 1024:1536].T)


def _diff_prep(p, g, cos, sin, s):
    bsz, n, _ = p.shape
    tm = ROW_TILE
    row = lambda b, i: (b, i, 0)
    const = lambda b, i: (0, 0)
    return pl.pallas_call(
        _diff_prep_kernel,
        grid=(bsz, n // tm),
        in_specs=[pl.BlockSpec((1, tm, P_DIFF), row),
                  pl.BlockSpec(g.shape, const),
                  pl.BlockSpec((tm, LANES), lambda b, i: (i, 0)),
                  pl.BlockSpec((tm, LANES), lambda b, i: (i, 0)),
                  pl.BlockSpec(s.shape, const)],
        out_specs=[pl.BlockSpec((1, tm, GROUP_W), row)] * 2 + [_VT_SPEC],
        out_shape=[jax.ShapeDtypeStruct((bsz, n, GROUP_W), BF16)] * 2 + [_vt_shape(bsz, n)],
        compiler_params=pltpu.CompilerParams(dimension_semantics=("parallel", "parallel"),
                                             vmem_limit_bytes=VMEM_LIMIT),
        name="diff_prep",
    )(p, g, cos, sin, s)


def _mla_prep_kernel(p_ref, qng_ref, kvng_ref, wuq_ref, wk_ref, wv_ref, gq_ref, gk_ref, gkr_ref, place_ref, s_ref,
                     cosa_ref, sina_ref, cosb_ref, sinb_ref, q_ref, k_ref, vt_ref):
    s = s_ref[...]
    half = MLA_ROPE // 2

    def rms(x, width):
        return x * lax.rsqrt(jnp.sum(x * x, axis=-1, keepdims=True) * (1.0 / width) + 1e-6)

    cqn = _bf(rms(p_ref[0, :, 0:MLA_Q_RANK], MLA_Q_RANK) * qng_ref[...])
    q = _dot(cqn, wuq_ref[...])
    cosa = jnp.concatenate([cosa_ref[...]] * MLA_HEADS, axis=1)
    sina = jnp.concatenate([sina_ref[...]] * MLA_HEADS, axis=1)
    qn = q * lax.rsqrt(_group_stat(q * q, s) + 1e-6) * gq_ref[...]
    qr = (qn * cosa + _swap_halves(qn, half) * sina) * ((MLA_NOPE + MLA_ROPE) ** -0.5 * LOG2E)
    q_ref[0] = _bf(qr)

    ckvn = _bf(rms(p_ref[0, :, MLA_Q_RANK:MLA_Q_RANK + MLA_KV_RANK], MLA_KV_RANK) * kvng_ref[...])
    kn = _dot(ckvn, wk_ref[...])
    knn = kn * lax.rsqrt(_group_stat(kn * kn, s) + 1e-6) * gk_ref[...]
    vt_ref[0, 0] = _bf(_dot(ckvn, wv_ref[...]).T)

    krn = rms(p_ref[0, :, 512:640], MLA_ROPE) * gkr_ref[...]
    krr = krn * cosb_ref[...] + _swap_halves(krn, half) * sinb_ref[...]
    k_ref[0] = _bf(knn + _dot(_bf(krr), place_ref[...]))


def _mla_prep(p, consts, tabs):
    bsz, n, _ = p.shape
    tm = ROW_TILE
    row = lambda b, i: (b, i, 0)
    const = lambda b, i: (0, 0)
    tab = pl.BlockSpec((tm, LANES), lambda b, i: (i, 0))
    qk_w = MLA_HEADS * LANES
    return pl.pallas_call(
        _mla_prep_kernel,
        grid=(bsz, n // tm),
        in_specs=[pl.BlockSpec((1, tm, P_MLA), row)] + [pl.BlockSpec(c.shape, const) for c in consts] + [tab] * 4,
        out_specs=[pl.BlockSpec((1, tm, qk_w), row), pl.BlockSpec((1, tm, qk_w), row), _VT_SPEC],
        out_shape=[jax.ShapeDtypeStruct((bsz, n, qk_w), BF16), jax.ShapeDtypeStruct((bsz, n, qk_w), BF16),
                   _vt_shape(bsz, n)],
        compiler_params=pltpu.CompilerParams(dimension_semantics=("parallel", "parallel"),
                                             vmem_limit_bytes=VMEM_LIMIT),
        name="mla_prep",
    )(p, *consts, *tabs)


def _attn_kernel(*refs, mode, n_keys, first_is_ctx, lam_init):
    if mode == "diff":
        q_ref, k_ref, vt_ref, lam_ref, g_ref, o_ref, s_sc = refs
    else:
        q_ref, k_ref, vt_ref, o_ref, s_sc = refs
    tq = q_ref.shape[1]
    qt = _bf(q_ref[0].astype(F32).T)
    row_lo = lax.broadcasted_iota(jnp.int32, (LANES, 1), 0) < HEAD_DIM
    if mode == "diff":
        zero = jnp.zeros_like(qt)
        qa, qb = jnp.where(row_lo, qt, zero), jnp.where(row_lo, zero, qt)
        v_dim, v_rows = LANES, (slice(None), slice(None))
    else:
        qa, qb = qt[:LANES], qt[LANES:]
        v_dim, v_rows = HEAD_DIM, (slice(0, HEAD_DIM), slice(HEAD_DIM, LANES))

    ones = jnp.ones((ONES_ROWS, ATTN_KB), BF16)

    def sweep(n_iter, nb):
        def scores(c, slot):
            out = []
            for half, qm in enumerate((qa, qb)):
                mx = None
                for j in range(nb):
                    kj = k_ref[0, pl.ds(pl.multiple_of((c * nb + j) * ATTN_KB, ATTN_KB), ATTN_KB), :]
                    if mode != "diff":
                        kj = kj[:, half * LANES:(half + 1) * LANES]
                    sj = _dot(kj, qm)
                    s_sc[slot, half, j * ATTN_KB:(j + 1) * ATTN_KB, :] = sj
                    mj = jnp.max(sj, axis=0, keepdims=True)
                    mx = mj if mx is None else jnp.maximum(mx, mj)
                out.append(mx)
            return tuple(out)

        def accumulate(c, slot, mxs, state):
            new = []
            for half, (mx, (m, acc)) in enumerate(zip(mxs, state)):
                m_new = jnp.maximum(m, mx)
                acc = jnp.exp2(m - m_new) * acc
                for j in range(nb):
                    p = jnp.exp2(s_sc[slot, half, j * ATTN_KB:(j + 1) * ATTN_KB, :] - m_new)
                    vj = jnp.concatenate([vt_ref[0, c * nb + j, v_rows[half], :], ones], axis=0)
                    acc = acc + _dot(vj, _bf(p))
                new.append((m_new, acc))
            return tuple(new)

        def body(c2, carry):
            mxs, state = carry
            c = 2 * c2
            mxs1 = scores(c + 1, 1)
            state = accumulate(c, 0, mxs, state)
            mxs0 = scores(c + 2, 0)
            state = accumulate(c + 1, 1, mxs1, state)
            return mxs0, state

        init = (jnp.full((1, tq), -1e30, F32), jnp.zeros((v_dim + ONES_ROWS, tq), F32))
        n_pairs = (n_iter - 1) // 2
        mxs, state = lax.fori_loop(0, n_pairs, body, (scores(0, 0), (init, init)),
                                   unroll=2 if n_pairs % 2 == 0 and n_pairs else 1)
        (_, acca), (_, accb) = accumulate(n_iter - 1, 0, mxs, state)
        return acca, accb

    def finish(carry):
        acca, accb = carry
        oa = acca[:v_dim] / acca[v_dim:v_dim + 1]
        ob = accb[:v_dim] / accb[v_dim:v_dim + 1]
        if mode == "diff":
            lam = lam_ref[...]
            lam_full = (jnp.exp(jnp.sum(lam[0:1] * lam[1:2], axis=-1, keepdims=True))
                        - jnp.exp(jnp.sum(lam[2:3] * lam[3:4], axis=-1, keepdims=True)) + lam_init)
            o = oa - lam_full * ob
            o = o * lax.rsqrt(jnp.mean(o * o, axis=0, keepdims=True) + 1e-6) * g_ref[...] * (1.0 - lam_init)
        else:
            o = jnp.concatenate([oa, ob], axis=0)
        o_ref[0] = _bf(o.T)

    n_blocks = n_keys // ATTN_KB
    if first_is_ctx:
        i = pl.program_id(2)

        @pl.when(i == 0)
        def _():
            finish(sweep(1, CTX_LEN // ATTN_KB))

        @pl.when(i > 0)
        def _():
            finish(sweep(n_blocks // ATTN_NB, ATTN_NB))
    else:
        finish(sweep(n_blocks // ATTN_NB, ATTN_NB))


def _attention(q, k, vt, extra, *, mode, want_ctx, lam_init=0.0):
    bsz, n, _ = k.shape
    tq = ROW_TILE
    off = 0 if want_ctx else 1
    n_out = n - off * tq
    qk_w = LANES if mode == "diff" else 2 * LANES
    in_specs = [pl.BlockSpec((1, tq, qk_w), lambda b, h, i: (b, i + off, h)),
                pl.BlockSpec((1, n, qk_w), lambda b, h, i: (b, 0, h)),
                pl.BlockSpec((1, n // ATTN_KB, LANES, ATTN_KB), lambda b, h, i: (b, 0, h, 0))]
    in_specs += [pl.BlockSpec(e.shape, lambda b, h, i: (0, 0)) for e in extra]
    return pl.pallas_call(
        functools.partial(_attn_kernel, mode=mode, n_keys=n, first_is_ctx=want_ctx, lam_init=lam_init),
        grid=(bsz, GROUP_W // LANES, n_out // tq),
        in_specs=in_specs,
        out_specs=pl.BlockSpec((1, tq, LANES), lambda b, h, i: (b, i, h)),
        out_shape=jax.ShapeDtypeStruct((bsz, n_out, GROUP_W), BF16),
        scratch_shapes=[pltpu.VMEM((2, 2, ATTN_NB * ATTN_KB, tq), F32)],
        compiler_params=pltpu.CompilerParams(dimension_semantics=("parallel", "parallel", "arbitrary"),
                                             vmem_limit_bytes=VMEM_LIMIT),
        name="attn_" + mode,
    )(q, k, vt, *extra)


def _halo_valid(i, n_tiles):
    return i >= 2, (i >= 1) & (i < n_tiles - 1)


def _rwkv_prep_kernel(p_ref, prev_ref, next_ref, sw_ref, w0_ref, wup_ref, a0_ref, aup_ref, gup_ref, kkw_ref, ka_ref,
                      ssum_ref, r_ref, k_ref, v_ref, kk_ref, lw_ref, a_ref, kd_ref, g_ref):
    i = pl.program_id(1)
    has_prev, has_next = _halo_valid(i, pl.num_programs(1))
    x = p_ref[0]
    prev = jnp.where(has_prev, prev_ref[0], 0.0)
    nxt = jnp.where(has_next, next_ref[0], 0.0)
    xm1, x0, xp1 = _shifted_rows(x, prev, nxt, (1, 0, -1))
    p = sw_ref[0:1, :] * xm1 + sw_ref[1:2, :] * x0 + sw_ref[2:3, :] * xp1

    r, k, v = p[:, 0:512], p[:, 512:1024], p[:, 1024:1536]
    w_raw = w0_ref[...] + _dot(_bf(jnp.tanh(p[:, 1536:1664])), wup_ref[...])
    lw = -RWKV_DECAY_SCALE * jax.nn.sigmoid(w_raw)
    a = jax.nn.sigmoid(a0_ref[...] + _dot(_bf(p[:, 1664:1792]), aup_ref[...]))
    kk = k * kkw_ref[...]
    kk = kk / jnp.maximum(jnp.sqrt(_group_stat(kk * kk, ssum_ref[...])), 1e-12)
    k2 = jnp.concatenate([k, k], axis=1)
    ka2 = jnp.concatenate([ka_ref[...]] * 2, axis=1)
    r_ref[0] = r
    k_ref[0] = k
    v_ref[0] = v
    kk_ref[0] = kk
    lw_ref[0] = lw
    a_ref[0] = a
    kd_ref[0] = k2 * (1.0 + (a - 1.0) * ka2)
    g_ref[0] = _dot(_bf(jax.nn.sigmoid(p[:, 1792:1920])), gup_ref[...])


def _halo_specs(width, tm):
    per = tm // HALO
    return [pl.BlockSpec((1, tm, width), lambda b, i: (b, i, 0)),
            pl.BlockSpec((1, HALO, width), lambda b, i: (b, jnp.maximum(i * per - 1, 0), 0)),
            pl.BlockSpec((1, HALO, width), lambda b, i: (b, jnp.minimum((i + 1) * per, pl.num_programs(1) * per - 1), 0))]


def _rwkv_prep(p, consts):
    bsz, n, _ = p.shape
    tm = ROW_TILE
    n_t = n // tm
    per = tm // HALO
    row = lambda b, i: (b, i, 0)
    const = lambda b, i: (0, 0)
    in_specs = [pl.BlockSpec((1, tm, P_RWKV), row),
                pl.BlockSpec((1, HALO, P_RWKV), lambda b, i: (b, jnp.maximum(i * per - 1, 0), 0)),
                pl.BlockSpec((1, HALO, P_RWKV), lambda b, i: (b, jnp.minimum((i + 1) * per, n_t * per - 1), 0))]
    in_specs += [pl.BlockSpec(c.shape, const) for c in consts]
    widths = (512, 512, 512, 512, 1024, 1024, 1024, 512)
    return pl.pallas_call(
        _rwkv_prep_kernel,
        grid=(bsz, n_t),
        in_specs=in_specs,
        out_specs=[pl.BlockSpec((1, tm, w), row) for w in widths],
        out_shape=[jax.ShapeDtypeStruct((bsz, n, w), F32) for w in widths],
        compiler_params=pltpu.CompilerParams(dimension_semantics=("parallel", "parallel"),
                                             vmem_limit_bytes=VMEM_LIMIT),
        name="rwkv_prep",
    )(p, p, p, *consts)


def _rwkv_scan_kernel(*refs, bsz):
    c = RWKV_CHUNK
    ins, y_refs, gt_sc = refs[:12], refs[12:14], refs[14]

    @pl.when(pl.program_id(0) == 0)
    def _():
        gt_sc[...] = jnp.zeros_like(gt_sc)

    ti = lax.broadcasted_iota(jnp.int32, (c, c), 0)
    si = lax.broadcasted_iota(jnp.int32, (c, c), 1)
    t2 = lax.broadcasted_iota(jnp.int32, (2 * c, 2 * c), 0)
    s2 = lax.broadcasted_iota(jnp.int32, (2 * c, 2 * c), 1)
    same_head = (t2 & c) == (s2 & c)
    eye = jnp.where(t2 == s2, 1.0, 0.0)
    m0 = lax.broadcasted_iota(jnp.int32, (1, LANES), 1) < HEAD_DIM

    def masks(fwd):
        tri = jnp.where((ti - si if fwd else si - ti) >= 0, 1.0, 0.0).astype(BF16)
        lag = jnp.where(same_head, t2 - s2 if fwd else s2 - t2, -1)
        return tri, lag > 0, lag >= 0

    def stack(x):
        return jnp.concatenate([jnp.where(m0, x, 0.0), jnp.where(m0, 0.0, x)], axis=0)

    chains = []
    for d in range(2):
        r_ref, kk_ref, v_ref, lw_ref, a_ref, kd_ref = ins[6 * d:6 * d + 6]
        tri, strict, incl = masks(d == 0)
        for b in range(bsz):
            for p in range(GROUP_W // LANES):
                sl = slice(p * LANES, (p + 1) * LANES)
                lw = lw_ref[b, :, sl]
                cum = _dot_exact_lhs(tri, lw)
                e_in, e_ex, e_neg = jnp.exp(cum), jnp.exp(cum - lw), jnp.exp(-cum)
                pe = jnp.exp(cum[c - 1:c, :] if d == 0 else cum[0:1, :])
                kk = kk_ref[b, :, sl]
                rb = stack(-(kk * a_ref[b, :, sl]) * e_neg)
                rk = stack(kd_ref[b, :, sl] * e_neg)
                vs = stack(v_ref[b, :, sl])
                chains.append(dict(
                    strict=strict, incl=incl, p_end=pe, vs=vs, vs_b=_bf(vs),
                    la=_bf(stack(kk * e_ex)), lr=_bf(stack(r_ref[b, :, sl] * e_in)), rb=_bf(rb), rk=_bf(rk),
                    rr=_bf(jnp.concatenate([rb * pe, rk * pe], axis=0)),
                    state=(d * bsz + b) * (GROUP_W // LANES) + p, out=(y_refs[d], b, sl)))

    for ch in chains:
        ch["a_ab"] = jnp.where(ch["strict"], _dot_nt(ch["la"], ch["rb"]), 0.0)
        ch["a_ak"] = _bf(jnp.where(ch["strict"], _dot_nt(ch["la"], ch["rk"]), 0.0))
        ch["a_rb"] = _bf(jnp.where(ch["incl"], _dot_nt(ch["lr"], ch["rb"]), 0.0))
        ch["a_rk"] = _bf(jnp.where(ch["incl"], _dot_nt(ch["lr"], ch["rk"]), 0.0))

    for ch in chains:
        ch["tm"] = eye + ch["a_ab"]
        ch["apow"] = ch["a_ab"]
    for _ in range(5):
        for ch in chains:
            ap = _bf(ch["apow"])
            ch["apow"] = _dot(ap, ap)
        for ch in chains:
            ch["tm"] = ch["tm"] + _dot(_bf(ch["tm"]), _bf(ch["apow"]))
    for ch in chains:
        ch["tm"] = _bf(ch["tm"])
        ch["akv"] = _bf(_dot(ch["a_ak"], ch["vs_b"]))
    for ch in chains:
        ch["w"] = _bf(_dot(ch["tm"], ch["la"]))
        ch["u0"] = _dot(ch["tm"], ch["akv"])
        ch["y0"] = _dot(ch["a_rk"], ch["vs_b"])

    for ch in chains:
        ch["gt"] = gt_sc[ch["state"]]
        ch["gt_b"] = _bf(ch["gt"])
        ch["u"] = _dot_nt(ch["w"], ch["gt_b"]) + ch["u0"]
    for ch in chains:
        y = _dot_nt(ch["lr"], ch["gt_b"]) + _dot(ch["a_rb"], _bf(ch["u"])) + ch["y0"]
        y_ref, b, sl = ch["out"]
        y_ref[b, :, sl] = y[:c] + y[c:]
        uv = jnp.concatenate([ch["u"], ch["vs"]], axis=0)
        gt_sc[ch["state"]] = ch["gt"] * ch["p_end"] + _dot(_bf(uv.T), ch["rr"])


def _bwd_chunk_index(s, n_ctx, n_all):
    return jnp.where(s < n_ctx, n_ctx - 1 - s, n_all + n_ctx - 1 - s)


def _scan_chunk_index(d, s, n_ctx, n_all):
    return jnp.where(d == 0, s, _bwd_chunk_index(s, n_ctx, n_all))


def _rwkv_scan(r, kk, v, lw, a, kd):
    bsz, n, _ = r.shape
    c = RWKV_CHUNK
    n_all, n_ctx = n // c, CTX_LEN // c
    blk = (bsz, c, GROUP_W)
    fwd = [pl.BlockSpec(blk, lambda s: (0, s, 0))] * 6
    bwd = ([pl.BlockSpec(blk, lambda s: (0, _bwd_chunk_index(s, n_ctx, n_all), 0))] * 3
           + [pl.BlockSpec(blk, lambda s: (0, _bwd_chunk_index(s, n_ctx, n_all), 1))] * 3)
    return pl.pallas_call(
        functools.partial(_rwkv_scan_kernel, bsz=bsz),
        grid=(n_all,),
        in_specs=fwd + bwd,
        out_specs=[pl.BlockSpec(blk, lambda s: (0, s, 0)),
                   pl.BlockSpec(blk, lambda s: (0, _bwd_chunk_index(s, n_ctx, n_all), 0))],
        out_shape=[jax.ShapeDtypeStruct((bsz, n, GROUP_W), F32)] * 2,
        scratch_shapes=[pltpu.VMEM((2 * bsz * (GROUP_W // LANES), LANES, LANES), F32)],
        compiler_params=pltpu.CompilerParams(dimension_semantics=("arbitrary",), vmem_limit_bytes=VMEM_LIMIT),
        name="rwkv_scan",
    )(r, kk, v, lw, a, kd, r, kk, v, lw, a, kd)


def _ssd_prep_kernel(p_ref, prev_ref, next_ref, cw_ref, cb_ref, dtb_ref, aneg_ref, e_ref, pa_ref,
                     xs_ref, bc_ref, xdt_ref, adt_ref):
    i = pl.program_id(1)
    has_prev, has_next = _halo_valid(i, pl.num_programs(1))
    x = p_ref[0, :, 512:1536]
    prev = jnp.where(has_prev, prev_ref[0, :, 512:1536], 0.0)
    nxt = jnp.where(has_next, next_ref[0, :, 512:1536], 0.0)
    taps = _shifted_rows(x, prev, nxt, (2, 1, 0, -1, -2))
    conv = cb_ref[...]
    for j, tap in enumerate(taps):
        conv = conv + cw_ref[j:j + 1, :] * tap
    xbc = _silu(conv)
    xs = xbc[:, 0:512]
    xs_ref[0] = xs
    bc_ref[0] = xbc[:, 512:1024]
    t = p_ref[0, :, 1536:1664] + dtb_ref[...]
    dt = jnp.maximum(t, 0.0) + jnp.log(1.0 + jnp.exp(-jnp.abs(t)))
    xdt_ref[0] = jnp.concatenate([xs, xs], axis=1) * _dot_exact_rhs(dt, e_ref[...])
    adt_ref[0] = _dot_exact_rhs(dt * aneg_ref[...], pa_ref[...])


def _ssd_prep(p, consts):
    bsz, n, _ = p.shape
    tm = ROW_TILE
    n_t = n // tm
    per = tm // HALO
    row = lambda b, i: (b, i, 0)
    const = lambda b, i: (0, 0)
    in_specs = [pl.BlockSpec((1, tm, P_SSD), row),
                pl.BlockSpec((1, HALO, P_SSD), lambda b, i: (b, jnp.maximum(i * per - 1, 0), 0)),
                pl.BlockSpec((1, HALO, P_SSD), lambda b, i: (b, jnp.minimum((i + 1) * per, n_t * per - 1), 0))]
    in_specs += [pl.BlockSpec(c.shape, const) for c in consts]
    widths = (512, 512, 1024, 256)
    return pl.pallas_call(
        _ssd_prep_kernel,
        grid=(bsz, n_t),
        in_specs=in_specs,
        out_specs=[pl.BlockSpec((1, tm, w), row) for w in widths],
        out_shape=[jax.ShapeDtypeStruct((bsz, n, w), F32) for w in widths],
        compiler_params=pltpu.CompilerParams(dimension_semantics=("parallel", "parallel"),
                                             vmem_limit_bytes=VMEM_LIMIT),
        name="ssd_prep",
    )(p, p, p, *consts)


def _ssd_scan_kernel(x_ref, a_ref, bc_ref, y_ref, ht_sc):
    q = SSD_CHUNK
    fwd = pl.program_id(0) == 0

    @pl.when(pl.program_id(2) == 0)
    def _():
        ht_sc[...] = jnp.zeros_like(ht_sc)

    ti = lax.broadcasted_iota(jnp.int32, (q, q), 0)
    si = lax.broadcasted_iota(jnp.int32, (q, q), 1)
    tri = jnp.where(fwd, ti - si, si - ti) >= 0
    acum = _dot_exact_lhs(jnp.where(tri, 1.0, 0.0).astype(BF16), a_ref[0])
    acum_t = acum.T
    last = jnp.where(fwd, acum[q - 1:q, :], acum[0:1, :])
    m0 = lax.broadcasted_iota(jnp.int32, (1, LANES), 1) < HEAD_DIM
    bc = bc_ref[0]

    def decay(h):
        seg = acum[:, h:h + 1] - acum_t[h:h + 1, :]
        return jnp.where(tri, jnp.exp(jnp.where(tri, seg, 0.0)), 0.0)

    def per_head(fn, h0):
        return jnp.where(m0, fn(h0), fn(h0 + 1))

    for g in range(2):
        bg = bc[:, g * SSD_STATE:(g + 1) * SSD_STATE]
        cg = bc[:, 256 + g * SSD_STATE:256 + (g + 1) * SSD_STATE]
        cg_b = _bf(cg)
        scores = _dot_nt(cg_b, _bf(bg))
        bg_t = _bf(bg.T)
        for pp in range(2):
            pair = 2 * g + pp
            h0 = 2 * pair
            sl = slice(pair * LANES, (pair + 1) * LANES)
            xp = x_ref[0, :, sl]
            xp_b = _bf(xp)
            y0 = _dot(_bf(scores * decay(h0)), xp_b)
            y1 = _dot(_bf(scores * decay(h0 + 1)), xp_b)
            ht = ht_sc[pair]
            carry = _dot(cg_b, _bf(ht)) * per_head(lambda h: jnp.exp(acum[:, h:h + 1]), h0)
            y_ref[0, 0, :, sl] = jnp.where(m0, y0, y1) + carry
            to_end = per_head(lambda h: jnp.exp(last[:, h:h + 1] - acum[:, h:h + 1]), h0)
            e_last = per_head(lambda h: jnp.exp(last[:, h:h + 1]), h0)
            ht_sc[pair] = ht * e_last + _dot(bg_t, _bf(xp * to_end))


def _ssd_scan(xdt, adt, bc):
    bsz, n, _ = bc.shape
    q = SSD_CHUNK
    n_all, n_ctx = n // q, CTX_LEN // q
    idx = lambda d, s: _scan_chunk_index(d, s, n_ctx, n_all)
    return pl.pallas_call(
        _ssd_scan_kernel,
        grid=(2, bsz, n_all),
        in_specs=[pl.BlockSpec((1, q, GROUP_W), lambda d, b, s: (b, idx(d, s), d)),
                  pl.BlockSpec((1, q, LANES), lambda d, b, s: (b, idx(d, s), d)),
                  pl.BlockSpec((1, q, GROUP_W), lambda d, b, s: (b, idx(d, s), 0))],
        out_specs=pl.BlockSpec((1, 1, q, GROUP_W), lambda d, b, s: (d, b, idx(d, s), 0)),
        out_shape=jax.ShapeDtypeStruct((2, bsz, n, GROUP_W), F32),
        scratch_shapes=[pltpu.VMEM((GROUP_W // LANES, SSD_STATE, LANES), F32)],
        compiler_params=pltpu.CompilerParams(dimension_semantics=("parallel", "parallel", "arbitrary"),
                                             vmem_limit_bytes=VMEM_LIMIT),
        name="ssd_scan",
    )(xdt, adt, bc)


def _post_kernel(z_ref, mctx_ref, mlat_ref, od_ref, om_ref, yrf_ref, yrb_ref, r_ref, k_ref, v_ref, g_ref,
                 lng_ref, lnb_ref, rk_ref, ysf_ref, ysb_ref, xs_ref, zg_ref, dsk_ref, ng_ref, smean_ref, ssum_ref,
                 w_ref, o_ref, *, tile_off):
    is_ctx = _is_ctx_rows(pl.program_id(1) + tile_off, ROW_TILE, CTX_LEN)
    smean = smean_ref[...]
    y = yrf_ref[0] + yrb_ref[0]
    yc = y - _group_stat(y, smean)
    yn = yc * lax.rsqrt(_group_stat(yc * yc, smean) + RWKV_LN_EPS) * lng_ref[...] + lnb_ref[...]
    bonus = _group_stat(r_ref[0] * k_ref[0] * rk_ref[...], ssum_ref[...]) * v_ref[0]
    o_rwkv = (yn + bonus) * g_ref[0]

    t = (ysf_ref[0, 0] + ysb_ref[0, 0] + dsk_ref[...] * xs_ref[0]) * _silu(zg_ref[0])
    o_ssd = t * lax.rsqrt(jnp.mean(t * t, axis=-1, keepdims=True) + 1e-6) * ng_ref[...]

    mix = (_dot(od_ref[0], w_ref[0:512, :]) + _dot(_bf(o_rwkv), w_ref[512:1024, :])
           + _dot(om_ref[0], w_ref[1024:1536, :]) + _dot(_bf(o_ssd), w_ref[1536:2048, :]))
    o_ref[0] = z_ref[0] + _mod_row(mctx_ref, mlat_ref, 5, is_ctx) * mix


def _post(z, m, od, om, yr, r, k, v, g, ys, xs, p_ssd, consts, w, *, want_ctx):
    bsz, n, d = z.shape
    tm = ROW_TILE
    off = 0 if want_ctx else 1
    n_out = n - off * tm
    full = lambda b, i: (b, i + off, 0)
    attn = lambda b, i: (b, i, 0)
    const = lambda b, i: (0, 0)
    seq = lambda width: pl.BlockSpec((1, tm, width), full)
    dirs = lambda dd: pl.BlockSpec((1, 1, tm, GROUP_W), lambda b, i: (dd, b, i + off, 0))
    in_specs = [seq(d),
                pl.BlockSpec((1, N_MOD, d), lambda b, i: (0, 0, 0)),
                pl.BlockSpec((1, N_MOD, d), lambda b, i: (b + 1, 0, 0)),
                pl.BlockSpec((1, tm, GROUP_W), attn), pl.BlockSpec((1, tm, GROUP_W), attn),
                seq(GROUP_W), seq(GROUP_W), seq(GROUP_W), seq(GROUP_W), seq(GROUP_W), seq(GROUP_W)]
    in_specs += [pl.BlockSpec(c.shape, const) for c in consts[:3]]
    in_specs += [dirs(0), dirs(1), seq(GROUP_W), seq(GROUP_W)]
    in_specs += [pl.BlockSpec(c.shape, const) for c in consts[3:]]
    in_specs += [pl.BlockSpec(w.shape, const, pipeline_mode=pl.Buffered(1))]
    return pl.pallas_call(
        functools.partial(_post_kernel, tile_off=off),
        grid=(bsz, n_out // tm),
        in_specs=in_specs,
        out_specs=pl.BlockSpec((1, tm, d), lambda b, i: (b, i, 0)),
        out_shape=jax.ShapeDtypeStruct((bsz, n_out, d), F32),
        compiler_params=pltpu.CompilerParams(dimension_semantics=("parallel", "parallel"),
                                             vmem_limit_bytes=VMEM_LIMIT),
        name="post",
    )(z, m, m, od, om, yr[0], yr[1], r, k, v, g, *consts[:3], ys, ys, xs, p_ssd, *consts[3:], w)


def _pad_cols(w, width):
    return jnp.pad(w, ((0, 0), (0, width - w.shape[1])))


def _block_diag_const(block, reps):
    return jnp.asarray(np.kron(np.eye(reps, dtype=np.float32), block), BF16)


def _group_consts():
    ones64 = np.ones((64, 64), np.float32)
    mla_blk = np.zeros((LANES, LANES), np.float32)
    mla_blk[:MLA_NOPE, :MLA_NOPE] = 1.0 / MLA_NOPE
    mla_blk[MLA_NOPE:MLA_NOPE + MLA_ROPE, MLA_NOPE:MLA_NOPE + MLA_ROPE] = 1.0 / MLA_ROPE
    return (_block_diag_const(ones64 / 64.0, 4), _block_diag_const(ones64, 4), _block_diag_const(mla_blk, 2))


def _rope_angles(n_lat, dim):
    quarter = dim // 4
    inv = ROPE_BASE ** (-jnp.arange(quarter, dtype=F32) / quarter)
    rows = n_lat // GRID_W
    pos_r = jnp.repeat(jnp.arange(rows), GRID_W).astype(F32)
    pos_c = jnp.tile(jnp.arange(GRID_W), rows).astype(F32)
    ang = jnp.concatenate([pos_r[:, None] * inv, pos_c[:, None] * inv], axis=-1)
    return jnp.cos(ang), jnp.sin(ang)


def _rope_lane_tables(n_lat, dim, lane_off):
    cos, sin = _rope_angles(n_lat, dim)
    cos_t = jnp.ones((n_lat, LANES), F32).at[:, lane_off:lane_off + dim].set(jnp.concatenate([cos, cos], axis=-1))
    sin_t = jnp.zeros((n_lat, LANES), F32).at[:, lane_off:lane_off + dim].set(jnp.concatenate([-sin, sin], axis=-1))
    cos_t = jnp.concatenate([jnp.ones((CTX_LEN, LANES), F32), cos_t], axis=0)
    sin_t = jnp.concatenate([jnp.zeros((CTX_LEN, LANES), F32), sin_t], axis=0)
    return cos_t, sin_t


def _row(x):
    return x.reshape(1, -1).astype(F32)


def _mla_consts(q_norm_g, kv_norm_g, w_uq, w_ukv, nope_g, rope_g, s_mla):
    hq = MLA_NOPE + MLA_ROPE
    w_uq = w_uq.reshape(MLA_Q_RANK, MLA_HEADS, hq)
    w_uq = jnp.pad(w_uq, ((0, 0), (0, 0), (0, LANES - hq))).reshape(MLA_Q_RANK, MLA_HEADS * LANES)
    w_ukv = w_ukv.reshape(MLA_KV_RANK, MLA_HEADS, 2 * MLA_NOPE)
    w_k = jnp.pad(w_ukv[:, :, :MLA_NOPE], ((0, 0), (0, 0), (0, LANES - MLA_NOPE))).reshape(MLA_KV_RANK, -1)
    w_v = w_ukv[:, :, MLA_NOPE:].reshape(MLA_KV_RANK, GROUP_W)
    zpad = jnp.zeros((LANES - hq,), F32)
    g_q = jnp.tile(jnp.concatenate([nope_g[0], rope_g[0], zpad]), MLA_HEADS)
    g_k = jnp.tile(jnp.concatenate([nope_g[1], jnp.zeros((LANES - MLA_NOPE,), F32)]), MLA_HEADS)
    g_kr = jnp.concatenate([rope_g[1], jnp.zeros((LANES - MLA_ROPE,), F32)])
    place = np.zeros((LANES, MLA_HEADS * LANES), np.float32)
    for h in range(MLA_HEADS):
        place[np.arange(MLA_ROPE), h * LANES + MLA_NOPE + np.arange(MLA_ROPE)] = 1.0
    return (_row(q_norm_g), _row(kv_norm_g), _bf(w_uq), _bf(w_k), _bf(w_v), _row(g_q), _row(g_k), _row(g_kr),
            jnp.asarray(place, BF16), s_mla)


def _two_dir_block_diag(w):
    z = jnp.zeros_like(w[0])
    return jnp.concatenate([jnp.concatenate([w[0], z], axis=1), jnp.concatenate([z, w[1]], axis=1)], axis=0)


def _ssd_expand_consts():
    e = np.zeros((LANES, 2 * GROUP_W), np.float32)
    pa = np.zeros((LANES, 2 * LANES), np.float32)
    for d in range(2):
        for h in range(SSD_HEADS):
            e[d * SSD_HEADS + h, d * GROUP_W + h * HEAD_DIM:d * GROUP_W + (h + 1) * HEAD_DIM] = 1.0
            pa[d * SSD_HEADS + h, d * LANES + h] = 1.0
    return jnp.asarray(e, BF16), jnp.asarray(pa, BF16)


def _pad_lanes(v):
    return jnp.pad(v.reshape(1, -1).astype(F32), ((0, 0), (0, LANES - v.size)))


def kernel(x, c, ctx, c_ctx, mod_w, mod_b, norm_g, ffn_w_gate, ffn_w_up, ffn_w_down, w_in, w_out, diff_qk_g, diff_lambda, diff_subln_g, rwkv_shift_w, rwkv_w0, rwkv_w_up, rwkv_a0, rwkv_a_up, rwkv_g_up, rwkv_k_k, rwkv_k_a, rwkv_r_k, rwkv_ln_g, rwkv_ln_b, mla_q_norm_g, mla_kv_norm_g, mla_w_uq, mla_w_ukv, mla_nope_g, mla_rope_g, ssd_conv_w, ssd_conv_b, ssd_dt_bias, ssd_a_log, ssd_d, ssd_norm_g):
    bsz, n_lat, d = x.shape
    depth = mod_w.shape[0]
    s_mean64, s_sum64, s_mla = _group_consts()
    e_dt, p_adt = _ssd_expand_consts()
    cos_d, sin_d = _rope_lane_tables(n_lat, HEAD_DIM, 0)
    cos_d = cos_d.at[:, HEAD_DIM:].set(cos_d[:, :HEAD_DIM])
    sin_d = sin_d.at[:, HEAD_DIM:].set(sin_d[:, :HEAD_DIM])
    cos_ma, sin_ma = _rope_lane_tables(n_lat, MLA_ROPE, MLA_NOPE)
    cos_mb, sin_mb = _rope_lane_tables(n_lat, MLA_ROPE, 0)

    cvec = jnp.zeros((8, d), F32).at[0].set(c_ctx).at[1:1 + bsz].set(c)
    z = jnp.concatenate([ctx, x], axis=1)
    tm_ffn = 640

    for l in range(depth):
        want_ctx = l < depth - 1
        lam_init = 0.8 - 0.6 * float(np.exp(-0.3 * l))
        m = _mod(cvec, mod_w[l], mod_b[l].reshape(1, N_MOD * d)).reshape(8, N_MOD, d)

        z = _ffn(z, m, _row(norm_g[l, 0]), _bf(ffn_w_gate[l, 0]), _bf(ffn_w_up[l, 0]), _bf(ffn_w_down[l, 0]),
                 j=0, ctx_len=CTX_LEN, tm=tm_ffn)

        w = w_in[l]
        w_p = jnp.concatenate([w[:, :3456], _pad_cols(w[:, 3456:4000], P_MLA), _pad_cols(w[:, 4000:], P_SSD)], axis=1)
        p_diff, p_rwkv, p_mla, p_ssd = _inproj(z, m, _row(norm_g[l, 1]), _bf(w_p))

        qd, kd, vd = _diff_prep(p_diff, jnp.tile(diff_qk_g[l], (1, 2 * DIFF_HEADS)), cos_d, sin_d, s_mean64)
        o_diff = _attention(qd, kd, vd, (diff_lambda[l], diff_subln_g[l].reshape(-1, 1)), mode="diff", want_ctx=want_ctx,
                            lam_init=lam_init)

        qm, km, vm = _mla_prep(p_mla, _mla_consts(mla_q_norm_g[l], mla_kv_norm_g[l], mla_w_uq[l], mla_w_ukv[l],
                                                  mla_nope_g[l], mla_rope_g[l], s_mla),
                               (cos_ma, sin_ma, cos_mb, sin_mb))
        o_mla = _attention(qm, km, vm, (), mode="mla", want_ctx=want_ctx)

        rw_consts = (rwkv_shift_w[l], _row(rwkv_w0[l]), _bf(_two_dir_block_diag(rwkv_w_up[l])), _row(rwkv_a0[l]),
                     _bf(_two_dir_block_diag(rwkv_a_up[l])), _bf(rwkv_g_up[l]), _row(rwkv_k_k[l]),
                     _row(rwkv_k_a[l]), s_sum64)
        r, k, v, kk, lw, a, kdir, g = _rwkv_prep(p_rwkv, rw_consts)
        y_rwkv = _rwkv_scan(r, kk, v, lw, a, kdir)

        sd_consts = (ssd_conv_w[l], _row(ssd_conv_b[l]), _pad_lanes(ssd_dt_bias[l]),
                     _pad_lanes(-jnp.exp(ssd_a_log[l])), e_dt, p_adt)
        xs, bc, xdt, adt = _ssd_prep(p_ssd, sd_consts)
        y_ssd = _ssd_scan(xdt, adt, bc)

        post_consts = (_row(rwkv_ln_g[l]), _row(rwkv_ln_b[l]), _row(rwkv_r_k[l]),
                       _row(jnp.repeat(ssd_d[l], HEAD_DIM)), _row(ssd_norm_g[l]), s_mean64, s_sum64)
        z = _post(z, m, o_diff, o_mla, y_rwkv, r, k, v, g, y_ssd, xs, p_ssd, post_consts, _bf(w_out[l]),
                  want_ctx=want_ctx)

        z = _ffn(z, m, _row(norm_g[l, 2]), _bf(ffn_w_gate[l, 1]), _bf(ffn_w_up[l, 1]), _bf(ffn_w_down[l, 1]),
                 j=2, ctx_len=CTX_LEN if want_ctx else 0, tm=tm_ffn if want_ctx else 512)
    return z
```

```python
import functools

import jax
import jax.numpy as jnp
import numpy as np
from jax import lax
from jax.experimental import pallas as pl
from jax.experimental.pallas import tpu as pltpu

F32 = jnp.float32
BF16 = jnp.bfloat16

CTX_LEN = 256
GRID_W = 64
HEAD_DIM = 64
ROPE_BASE = 10000.0
N_MOD = 9
GROUP_W = 512
DIFF_HEADS = 4
MLA_HEADS = 8
MLA_NOPE = 64
MLA_ROPE = 32
MLA_Q_RANK = 384
MLA_KV_RANK = 128
RWKV_DECAY_SCALE = 0.606531
RWKV_LN_EPS = 64e-5
RWKV_CHUNK = 64
SSD_HEADS = 8
SSD_STATE = 128
SSD_CHUNK = 128
LOG2E = 1.4426950408889634

LANES = 128
ROW_TILE = CTX_LEN
ADALN_ROWS = 16
HALO = 8
ATTN_KB = ROW_TILE
ATTN_NB = 5
ONES_ROWS = 16
VMEM_LIMIT = 56 * 1024 * 1024

P_DIFF = 1536
P_RWKV = 1920
P_MLA = 640
P_SSD = 1664
P_OFFS = (0, P_DIFF, P_DIFF + P_RWKV, P_DIFF + P_RWKV + P_MLA, P_DIFF + P_RWKV + P_MLA + P_SSD)


def _dot(a, b):
    return jnp.dot(a, b, preferred_element_type=F32)


def _dot_nt(a, b):
    return lax.dot_general(a, b, (((1,), (1,)), ((), ())), preferred_element_type=F32)


def _bf(x):
    return x.astype(BF16)


def _split3(x):
    x1 = x.astype(BF16)
    r1 = x - x1.astype(F32)
    x2 = r1.astype(BF16)
    r2 = r1 - x2.astype(F32)
    return x1, x2, r2.astype(BF16)


def _dot_exact_lhs(a, x):
    x1, x2, x3 = _split3(x)
    return _dot(a, x1) + _dot(a, x2) + _dot(a, x3)


def _dot_exact_rhs(x, a):
    x1, x2, x3 = _split3(x)
    return _dot(x1, a) + _dot(x2, a) + _dot(x3, a)


def _group_stat(x, s):
    hi = x.astype(BF16)
    lo = (x - hi.astype(F32)).astype(BF16)
    blk = s.shape[0]
    outs = []
    for c in range(x.shape[-1] // blk):
        sl = slice(c * blk, (c + 1) * blk)
        outs.append(_dot(hi[:, sl], s) + _dot(lo[:, sl], s))
    return outs[0] if len(outs) == 1 else jnp.concatenate(outs, axis=-1)


def _swap_halves(x, half):
    n = x.shape[-1]
    lane = lax.broadcasted_iota(jnp.int32, (1, n), 1)
    up = pltpu.roll(x, n - half, axis=1)
    dn = pltpu.roll(x, half, axis=1)
    return jnp.where((lane & (2 * half - 1)) < half, up, dn)


def _silu(x):
    return x * jax.nn.sigmoid(x)


def _is_ctx_rows(tile_idx, tm, ctx_len):
    rows = tile_idx * tm + lax.broadcasted_iota(jnp.int32, (tm, 1), 0)
    return rows < ctx_len


def _mod_row(mctx_ref, mlat_ref, idx, is_ctx):
    return jnp.where(is_ctx, mctx_ref[0, idx:idx + 1, :], mlat_ref[0, idx:idx + 1, :])


def _adaln(z, g, mctx_ref, mlat_ref, j, is_ctx):
    ms = jnp.mean(z * z, axis=-1, keepdims=True)
    h = z * lax.rsqrt(ms + 1e-6) * g
    return h * (1.0 + _mod_row(mctx_ref, mlat_ref, 3 * j + 1, is_ctx)) + _mod_row(mctx_ref, mlat_ref, 3 * j, is_ctx)


def _adaln_to_scratch(z_ref, g_ref, mctx_ref, mlat_ref, j, tile_idx, tm, ctx_len, h_sc):
    assert ctx_len % ADALN_ROWS == 0 and tm % ADALN_ROWS == 0
    g = g_ref[...]

    def rows(r, carry):
        sl = pl.ds(pl.multiple_of(r * ADALN_ROWS, ADALN_ROWS), ADALN_ROWS)
        step_is_ctx = tile_idx * tm + r * ADALN_ROWS < ctx_len
        h_sc[sl, :] = _bf(_adaln(z_ref[0, sl, :], g, mctx_ref, mlat_ref, j, step_is_ctx))
        return carry

    lax.fori_loop(0, tm // ADALN_ROWS, rows, 0, unroll=4)


def _shifted_rows(x, prev, nxt, shifts):
    tm = x.shape[0]
    ext = jnp.concatenate([prev, x, nxt], axis=0)
    n = tm + 2 * HALO
    out = []
    for s in shifts:
        out.append(x if s == 0 else pltpu.roll(ext, s % n, axis=0)[HALO:HALO + tm])
    return out


def _mod_kernel(c_ref, w_ref, b_ref, o_ref):
    o_ref[...] = _dot(_bf(_silu(c_ref[...])), _bf(w_ref[...])) + b_ref[...]


def _mod(cvec, w, b):
    d, n = w.shape
    tn = 1024
    return pl.pallas_call(
        _mod_kernel,
        grid=(n // tn,),
        in_specs=[pl.BlockSpec((8, d), lambda i: (0, 0)),
                  pl.BlockSpec((d, tn), lambda i: (0, i)),
                  pl.BlockSpec((1, tn), lambda i: (0, i))],
        out_specs=pl.BlockSpec((8, tn), lambda i: (0, i)),
        out_shape=jax.ShapeDtypeStruct((8, n), F32),
        compiler_params=pltpu.CompilerParams(dimension_semantics=("parallel",), vmem_limit_bytes=VMEM_LIMIT),
        name="mod",
    )(cvec, w, b)


def _ffn_kernel(z_ref, mctx_ref, mlat_ref, g_ref, wg_ref, wu_ref, wd_ref, o_ref, h_sc, acc_sc, *, j, ctx_len, tm):
    f = pl.program_id(2)

    @pl.when(f == 0)
    def _():
        _adaln_to_scratch(z_ref, g_ref, mctx_ref, mlat_ref, j, pl.program_id(1), tm, ctx_len, h_sc)
        acc_sc[...] = jnp.zeros_like(acc_sc)

    h = h_sc[...]
    act = _silu(_dot(h, wg_ref[...])) * _dot(h, wu_ref[...])
    acc_sc[...] += _dot(_bf(act), wd_ref[...])

    @pl.when(f == pl.num_programs(2) - 1)
    def _():
        gate = _mod_row(mctx_ref, mlat_ref, 3 * j + 2, _is_ctx_rows(pl.program_id(1), tm, ctx_len))
        o_ref[0] = z_ref[0] + gate * (0.5 * acc_sc[...])


def _ffn(z, m, g, wg, wu, wd, *, j, ctx_len, tm):
    bsz, n, d = z.shape
    ff = wg.shape[1]
    tf = 512
    return pl.pallas_call(
        functools.partial(_ffn_kernel, j=j, ctx_len=ctx_len, tm=tm),
        grid=(bsz, n // tm, ff // tf),
        in_specs=[pl.BlockSpec((1, tm, d), lambda b, i, f: (b, i, 0)),
                  pl.BlockSpec((1, N_MOD, d), lambda b, i, f: (0, 0, 0)),
                  pl.BlockSpec((1, N_MOD, d), lambda b, i, f: (b + 1, 0, 0)),
                  pl.BlockSpec((1, d), lambda b, i, f: (0, 0)),
                  pl.BlockSpec((d, tf), lambda b, i, f: (0, f)),
                  pl.BlockSpec((d, tf), lambda b, i, f: (0, f)),
                  pl.BlockSpec((tf, d), lambda b, i, f: (f, 0))],
        out_specs=pl.BlockSpec((1, tm, d), lambda b, i, f: (b, i, 0)),
        out_shape=jax.ShapeDtypeStruct(z.shape, F32),
        scratch_shapes=[pltpu.VMEM((tm, d), BF16), pltpu.VMEM((tm, d), F32)],
        compiler_params=pltpu.CompilerParams(dimension_semantics=("parallel", "parallel", "arbitrary"),
                                             vmem_limit_bytes=VMEM_LIMIT),
        name="ffn",
    )(z, m, m, g, wg, wu, wd)


def _inproj_kernel(z_ref, mctx_ref, mlat_ref, g_ref, w_ref, od_ref, or_ref, om_ref, os_ref, h_sc):
    _adaln_to_scratch(z_ref, g_ref, mctx_ref, mlat_ref, 1, pl.program_id(1), ROW_TILE, CTX_LEN, h_sc)
    h = h_sc[...]
    for o_ref, lo, hi in zip((od_ref, or_ref, om_ref, os_ref), P_OFFS[:-1], P_OFFS[1:]):
        o_ref[0] = _dot(h, w_ref[:, lo:hi])


def _inproj(z, m, g, w):
    bsz, n, d = z.shape
    tm = ROW_TILE
    widths = (P_DIFF, P_RWKV, P_MLA, P_SSD)
    return pl.pallas_call(
        _inproj_kernel,
        grid=(bsz, n // tm),
        in_specs=[pl.BlockSpec((1, tm, d), lambda b, i: (b, i, 0)),
                  pl.BlockSpec((1, N_MOD, d), lambda b, i: (0, 0, 0)),
                  pl.BlockSpec((1, N_MOD, d), lambda b, i: (b + 1, 0, 0)),
                  pl.BlockSpec((1, d), lambda b, i: (0, 0)),
                  pl.BlockSpec(w.shape, lambda b, i: (0, 0), pipeline_mode=pl.Buffered(1))],
        out_specs=[pl.BlockSpec((1, tm, wd), lambda b, i: (b, i, 0)) for wd in widths],
        out_shape=[jax.ShapeDtypeStruct((bsz, n, wd), F32) for wd in widths],
        scratch_shapes=[pltpu.VMEM((tm, d), BF16)],
        compiler_params=pltpu.CompilerParams(dimension_semantics=("parallel", "parallel"),
                                             vmem_limit_bytes=VMEM_LIMIT),
        name="inproj",
    )(z, m, m, g, w)


_VT_SPEC = pl.BlockSpec((1, 1, GROUP_W, ROW_TILE), lambda b, i: (b, i, 0, 0))


def _vt_shape(bsz, n):
    return jax.ShapeDtypeStruct((bsz, n // ROW_TILE, GROUP_W, ROW_TILE), BF16)


def _diff_prep_kernel(p_ref, g_ref, cos_ref, sin_ref, s_ref, q_ref, k_ref, vt_ref):
    s = s_ref[...]
    cos = jnp.concatenate([cos_ref[...]] * 4, axis=1)
    sin = jnp.concatenate([sin_ref[...]] * 4, axis=1)

    def norm_rope(x, g):
        xn = x * lax.rsqrt(_group_stat(x * x, s) + 1e-6) * g
        return xn * cos + _swap_halves(xn, HEAD_DIM // 2) * sin

    q = norm_rope(p_ref[0, :, 0:512], g_ref[0:1, :]) * (HEAD_DIM ** -0.5 * LOG2E)
    k = norm_rope(p_ref[0, :, 512:1024], g_ref[1:2, :])
    q_ref[0] = _bf(q)
    k_ref[0] = _bf(k)
    vt_ref[0, 0] = _bf(p_ref[0, :, 1024:1536].T)


def _diff_prep(p, g, cos, sin, s):
    bsz, n, _ = p.shape
    tm = ROW_TILE
    row = lambda b, i: (b, i, 0)
    const = lambda b, i: (0, 0)
    return pl.pallas_call(
        _diff_prep_kernel,
        grid=(bsz, n // tm),
        in_specs=[pl.BlockSpec((1, tm, P_DIFF), row),
                  pl.BlockSpec(g.shape, const),
                  pl.BlockSpec((tm, LANES), lambda b, i: (i, 0)),
                  pl.BlockSpec((tm, LANES), lambda b, i: (i, 0)),
                  pl.BlockSpec(s.shape, const)],
        out_specs=[pl.BlockSpec((1, tm, GROUP_W), row)] * 2 + [_VT_SPEC],
        out_shape=[jax.ShapeDtypeStruct((bsz, n, GROUP_W), BF16)] * 2 + [_vt_shape(bsz, n)],
        compiler_params=pltpu.CompilerParams(dimension_semantics=("parallel", "parallel"),
                                             vmem_limit_bytes=VMEM_LIMIT),
        name="diff_prep",
    )(p, g, cos, sin, s)


def _mla_prep_kernel(p_ref, qng_ref, kvng_ref, wuq_ref, wk_ref, wv_ref, gq_ref, gk_ref, gkr_ref, place_ref, s_ref,
                     cosa_ref, sina_ref, cosb_ref, sinb_ref, q_ref, k_ref, vt_ref):
    s = s_ref[...]
    half = MLA_ROPE // 2

    def rms(x, width):
        return x * lax.rsqrt(jnp.sum(x * x, axis=-1, keepdims=True) * (1.0 / width) + 1e-6)

    cqn = _bf(rms(p_ref[0, :, 0:MLA_Q_RANK], MLA_Q_RANK) * qng_ref[...])
    q = _dot(cqn, wuq_ref[...])
    cosa = jnp.concatenate([cosa_ref[...]] * MLA_HEADS, axis=1)
    sina = jnp.concatenate([sina_ref[...]] * MLA_HEADS, axis=1)
    qn = q * lax.rsqrt(_group_stat(q * q, s) + 1e-6) * gq_ref[...]
    qr = (qn * cosa + _swap_halves(qn, half) * sina) * ((MLA_NOPE + MLA_ROPE) ** -0.5 * LOG2E)
    q_ref[0] = _bf(qr)

    ckvn = _bf(rms(p_ref[0, :, MLA_Q_RANK:MLA_Q_RANK + MLA_KV_RANK], MLA_KV_RANK) * kvng_ref[...])
    kn = _dot(ckvn, wk_ref[...])
    knn = kn * lax.rsqrt(_group_stat(kn * kn, s) + 1e-6) * gk_ref[...]
    vt_ref[0, 0] = _bf(_dot(ckvn, wv_ref[...]).T)

    krn = rms(p_ref[0, :, 512:640], MLA_ROPE) * gkr_ref[...]
    krr = krn * cosb_ref[...] + _swap_halves(krn, half) * sinb_ref[...]
    k_ref[0] = _bf(knn + _dot(_bf(krr), place_ref[...]))


def _mla_prep(p, consts, tabs):
    bsz, n, _ = p.shape
    tm = ROW_TILE
    row = lambda b, i: (b, i, 0)
    const = lambda b, i: (0, 0)
    tab = pl.BlockSpec((tm, LANES), lambda b, i: (i, 0))
    qk_w = MLA_HEADS * LANES
    return pl.pallas_call(
        _mla_prep_kernel,
        grid=(bsz, n // tm),
        in_specs=[pl.BlockSpec((1, tm, P_MLA), row)] + [pl.BlockSpec(c.shape, const) for c in consts] + [tab] * 4,
        out_specs=[pl.BlockSpec((1, tm, qk_w), row), pl.BlockSpec((1, tm, qk_w), row), _VT_SPEC],
        out_shape=[jax.ShapeDtypeStruct((bsz, n, qk_w), BF16), jax.ShapeDtypeStruct((bsz, n, qk_w), BF16),
                   _vt_shape(bsz, n)],
        compiler_params=pltpu.CompilerParams(dimension_semantics=("parallel", "parallel"),
                                             vmem_limit_bytes=VMEM_LIMIT),
        name="mla_prep",
    )(p, *consts, *tabs)


def _attn_kernel(*refs, mode, n_keys, first_is_ctx, lam_init):
    if mode == "diff":
        q_ref, k_ref, vt_ref, lam_ref, g_ref, o_ref, s_sc = refs
    else:
        q_ref, k_ref, vt_ref, o_ref, s_sc = refs
    tq = q_ref.shape[1]
    qt = _bf(q_ref[0].astype(F32).T)
    row_lo = lax.broadcasted_iota(jnp.int32, (LANES, 1), 0) < HEAD_DIM
    if mode == "diff":
        zero = jnp.zeros_like(qt)
        qa, qb = jnp.where(row_lo, qt, zero), jnp.where(row_lo, zero, qt)
        v_dim, v_rows = LANES, (slice(None), slice(None))
    else:
        qa, qb = qt[:LANES], qt[LANES:]
        v_dim, v_rows = HEAD_DIM, (slice(0, HEAD_DIM), slice(HEAD_DIM, LANES))

    def sweep(n_iter, nb):
        def scores(c, slot):
            out = []
            rows = nb * ATTN_KB
            kc = k_ref[0, pl.ds(pl.multiple_of(c * rows, rows), rows), :]
            for half, qm in enumerate((qa, qb)):
                s = _dot(kc if mode == "diff" else kc[:, half * LANES:(half + 1) * LANES], qm)
                s_sc[slot, half, 0:rows, :] = s
                out.append(jnp.max(s, axis=0, keepdims=True))
            return tuple(out)

        def accumulate(c, slot, mxs, state):
            new = []
            for half, (mx, (m, acc)) in enumerate(zip(mxs, state)):
                m_new = jnp.maximum(m, mx)
                p = jnp.exp2(s_sc[slot, half, 0:nb * ATTN_KB, :] - m_new)
                vg = jnp.concatenate([vt_ref[0, c * nb + j, v_rows[half], :] for j in range(nb)], axis=1)
                vg = jnp.concatenate([vg, jnp.ones((ONES_ROWS, nb * ATTN_KB), BF16)], axis=0)
                acc = jnp.exp2(m - m_new) * acc + _dot(vg, _bf(p))
                new.append((m_new, acc))
            return tuple(new)

        def body(c2, carry):
            mxs, state = carry
            c = 2 * c2
            mxs1 = scores(c + 1, 1)
            state = accumulate(c, 0, mxs, state)
            mxs0 = scores(c + 2, 0)
            state = accumulate(c + 1, 1, mxs1, state)
            return mxs0, state

        init = (jnp.full((1, tq), -1e30, F32), jnp.zeros((v_dim + ONES_ROWS, tq), F32))
        n_pairs = (n_iter - 1) // 2
        mxs, state = lax.fori_loop(0, n_pairs, body, (scores(0, 0), (init, init)),
                                   unroll=2 if n_pairs % 2 == 0 and n_pairs else 1)
        (_, acca), (_, accb) = accumulate(n_iter - 1, 0, mxs, state)
        return acca, accb


    def finish(carry):
        acca, accb = carry
        oa = acca[:v_dim] / acca[v_dim:v_dim + 1]
        ob = accb[:v_dim] / accb[v_dim:v_dim + 1]
        if mode == "diff":
            lam = lam_ref[...]
            lam_full = (jnp.exp(jnp.sum(lam[0:1] * lam[1:2], axis=-1, keepdims=True))
                        - jnp.exp(jnp.sum(lam[2:3] * lam[3:4], axis=-1, keepdims=True)) + lam_init)
            o = oa - lam_full * ob
            o = o * lax.rsqrt(jnp.mean(o * o, axis=0, keepdims=True) + 1e-6) * g_ref[...] * (1.0 - lam_init)
        else:
            o = jnp.concatenate([oa, ob], axis=0)
        o_ref[0] = _bf(o.T)

    n_blocks = n_keys // ATTN_KB
    if first_is_ctx:
        i = pl.program_id(2)

        @pl.when(i == 0)
        def _():
            finish(sweep(1, CTX_LEN // ATTN_KB))

        @pl.when(i > 0)
        def _():
            finish(sweep(n_blocks // ATTN_NB, ATTN_NB))
    else:
        finish(sweep(n_blocks // ATTN_NB, ATTN_NB))


def _attention(q, k, vt, extra, *, mode, want_ctx, lam_init=0.0):
    bsz, n, _ = k.shape
    tq = ROW_TILE
    off = 0 if want_ctx else 1
    n_out = n - off * tq
    qk_w = LANES if mode == "diff" else 2 * LANES
    in_specs = [pl.BlockSpec((1, tq, qk_w), lambda b, h, i: (b, i + off, h)),
                pl.BlockSpec((1, n, qk_w), lambda b, h, i: (b, 0, h)),
                pl.BlockSpec((1, n // ATTN_KB, LANES, ATTN_KB), lambda b, h, i: (b, 0, h, 0))]
    in_specs += [pl.BlockSpec(e.shape, lambda b, h, i: (0, 0)) for e in extra]
    return pl.pallas_call(
        functools.partial(_attn_kernel, mode=mode, n_keys=n, first_is_ctx=want_ctx, lam_init=lam_init),
        grid=(bsz, GROUP_W // LANES, n_out // tq),
        in_specs=in_specs,
        out_specs=pl.BlockSpec((1, tq, LANES), lambda b, h, i: (b, i, h)),
        out_shape=jax.ShapeDtypeStruct((bsz, n_out, GROUP_W), BF16),
        scratch_shapes=[pltpu.VMEM((2, 2, ATTN_NB * ATTN_KB, tq), F32)],
        compiler_params=pltpu.CompilerParams(dimension_semantics=("parallel", "parallel", "arbitrary"),
                                             vmem_limit_bytes=VMEM_LIMIT),
        name="attn_" + mode,
    )(q, k, vt, *extra)


def _halo_valid(i, n_tiles):
    return i >= 2, (i >= 1) & (i < n_tiles - 1)


def _rwkv_prep_kernel(p_ref, prev_ref, next_ref, sw_ref, w0_ref, wup_ref, a0_ref, aup_ref, gup_ref, kkw_ref, ka_ref,
                      ssum_ref, r_ref, k_ref, v_ref, kk_ref, lw_ref, a_ref, kd_ref, g_ref):
    i = pl.program_id(1)
    has_prev, has_next = _halo_valid(i, pl.num_programs(1))
    x = p_ref[0]
    prev = jnp.where(has_prev, prev_ref[0], 0.0)
    nxt = jnp.where(has_next, next_ref[0], 0.0)
    xm1, x0, xp1 = _shifted_rows(x, prev, nxt, (1, 0, -1))
    p = sw_ref[0:1, :] * xm1 + sw_ref[1:2, :] * x0 + sw_ref[2:3, :] * xp1

    r, k, v = p[:, 0:512], p[:, 512:1024], p[:, 1024:1536]
    w_raw = w0_ref[...] + _dot(_bf(jnp.tanh(p[:, 1536:1664])), wup_ref[...])
    lw = -RWKV_DECAY_SCALE * jax.nn.sigmoid(w_raw)
    a = jax.nn.sigmoid(a0_ref[...] + _dot(_bf(p[:, 1664:1792]), aup_ref[...]))
    kk = k * kkw_ref[...]
    kk = kk / jnp.maximum(jnp.sqrt(_group_stat(kk * kk, ssum_ref[...])), 1e-12)
    k2 = jnp.concatenate([k, k], axis=1)
    ka2 = jnp.concatenate([ka_ref[...]] * 2, axis=1)
    r_ref[0] = r
    k_ref[0] = k
    v_ref[0] = v
    kk_ref[0] = kk
    lw_ref[0] = lw
    a_ref[0] = a
    kd_ref[0] = k2 * (1.0 + (a - 1.0) * ka2)
    g_ref[0] = _dot(_bf(jax.nn.sigmoid(p[:, 1792:1920])), gup_ref[...])


def _halo_specs(width, tm):
    per = tm // HALO
    return [pl.BlockSpec((1, tm, width), lambda b, i: (b, i, 0)),
            pl.BlockSpec((1, HALO, width), lambda b, i: (b, jnp.maximum(i * per - 1, 0), 0)),
            pl.BlockSpec((1, HALO, width), lambda b, i: (b, jnp.minimum((i + 1) * per, pl.num_programs(1) * per - 1), 0))]


def _rwkv_prep(p, consts):
    bsz, n, _ = p.shape
    tm = ROW_TILE
    n_t = n // tm
    per = tm // HALO
    row = lambda b, i: (b, i, 0)
    const = lambda b, i: (0, 0)
    in_specs = [pl.BlockSpec((1, tm, P_RWKV), row),
                pl.BlockSpec((1, HALO, P_RWKV), lambda b, i: (b, jnp.maximum(i * per - 1, 0), 0)),
                pl.BlockSpec((1, HALO, P_RWKV), lambda b, i: (b, jnp.minimum((i + 1) * per, n_t * per - 1), 0))]
    in_specs += [pl.BlockSpec(c.shape, const) for c in consts]
    widths = (512, 512, 512, 512, 1024, 1024, 1024, 512)
    return pl.pallas_call(
        _rwkv_prep_kernel,
        grid=(bsz, n_t),
        in_specs=in_specs,
        out_specs=[pl.BlockSpec((1, tm, w), row) for w in widths],
        out_shape=[jax.ShapeDtypeStruct((bsz, n, w), F32) for w in widths],
        compiler_params=pltpu.CompilerParams(dimension_semantics=("parallel", "parallel"),
                                             vmem_limit_bytes=VMEM_LIMIT),
        name="rwkv_prep",
    )(p, p, p, *consts)


def _rwkv_scan_kernel(*refs, bsz):
    c = RWKV_CHUNK
    ins, y_refs, gt_sc = refs[:12], refs[12:14], refs[14]

    @pl.when(pl.program_id(0) == 0)
    def _():
        gt_sc[...] = jnp.zeros_like(gt_sc)

    ti = lax.broadcasted_iota(jnp.int32, (c, c), 0)
    si = lax.broadcasted_iota(jnp.int32, (c, c), 1)
    t2 = lax.broadcasted_iota(jnp.int32, (2 * c, 2 * c), 0)
    s2 = lax.broadcasted_iota(jnp.int32, (2 * c, 2 * c), 1)
    same_head = (t2 & c) == (s2 & c)
    eye = jnp.where(t2 == s2, 1.0, 0.0)
    m0 = lax.broadcasted_iota(jnp.int32, (1, LANES), 1) < HEAD_DIM

    def masks(fwd):
        tri = jnp.where((ti - si if fwd else si - ti) >= 0, 1.0, 0.0).astype(BF16)
        lag = jnp.where(same_head, t2 - s2 if fwd else s2 - t2, -1)
        return tri, lag > 0, lag >= 0

    def stack(x):
        return jnp.concatenate([jnp.where(m0, x, 0.0), jnp.where(m0, 0.0, x)], axis=0)

    chains = []
    for d in range(2):
        r_ref, kk_ref, v_ref, lw_ref, a_ref, kd_ref = ins[6 * d:6 * d + 6]
        tri, strict, incl = masks(d == 0)
        for b in range(bsz):
            for p in range(GROUP_W // LANES):
                sl = slice(p * LANES, (p + 1) * LANES)
                lw = lw_ref[b, :, sl]
                cum = _dot_exact_lhs(tri, lw)
                e_in, e_ex, e_neg = jnp.exp(cum), jnp.exp(cum - lw), jnp.exp(-cum)
                pe = jnp.exp(cum[c - 1:c, :] if d == 0 else cum[0:1, :])
                kk = kk_ref[b, :, sl]
                rb = stack(-(kk * a_ref[b, :, sl]) * e_neg)
                rk = stack(kd_ref[b, :, sl] * e_neg)
                vs = stack(v_ref[b, :, sl])
                chains.append(dict(
                    strict=strict, incl=incl, p_end=pe, vs=vs, vs_b=_bf(vs),
                    la=_bf(stack(kk * e_ex)), lr=_bf(stack(r_ref[b, :, sl] * e_in)), rb=_bf(rb), rk=_bf(rk),
                    rr=_bf(jnp.concatenate([rb * pe, rk * pe], axis=0)),
                    state=(d * bsz + b) * (GROUP_W // LANES) + p, out=(y_refs[d], b, sl)))

    for ch in chains:
        ch["a_ab"] = jnp.where(ch["strict"], _dot_nt(ch["la"], ch["rb"]), 0.0)
        ch["a_ak"] = _bf(jnp.where(ch["strict"], _dot_nt(ch["la"], ch["rk"]), 0.0))
        ch["a_rb"] = _bf(jnp.where(ch["incl"], _dot_nt(ch["lr"], ch["rb"]), 0.0))
        ch["a_rk"] = _bf(jnp.where(ch["incl"], _dot_nt(ch["lr"], ch["rk"]), 0.0))

    for ch in chains:
        ch["tm"] = eye + ch["a_ab"]
        ch["apow"] = ch["a_ab"]
    for _ in range(5):
        for ch in chains:
            ap = _bf(ch["apow"])
            ch["apow"] = _dot(ap, ap)
        for ch in chains:
            ch["tm"] = ch["tm"] + _dot(_bf(ch["tm"]), _bf(ch["apow"]))
    for ch in chains:
        ch["tm"] = _bf(ch["tm"])
        ch["akv"] = _bf(_dot(ch["a_ak"], ch["vs_b"]))
    for ch in chains:
        ch["w"] = _bf(_dot(ch["tm"], ch["la"]))
        ch["u0"] = _dot(ch["tm"], ch["akv"])
        ch["y0"] = _dot(ch["a_rk"], ch["vs_b"])

    for ch in chains:
        ch["gt"] = gt_sc[ch["state"]]
        ch["gt_b"] = _bf(ch["gt"])
        ch["u"] = _dot_nt(ch["w"], ch["gt_b"]) + ch["u0"]
    for ch in chains:
        y = _dot_nt(ch["lr"], ch["gt_b"]) + _dot(ch["a_rb"], _bf(ch["u"])) + ch["y0"]
        y_ref, b, sl = ch["out"]
        y_ref[b, :, sl] = y[:c] + y[c:]
        uv = jnp.concatenate([ch["u"], ch["vs"]], axis=0)
        gt_sc[ch["state"]] = ch["gt"] * ch["p_end"] + _dot(_bf(uv.T), ch["rr"])


def _bwd_chunk_index(s, n_ctx, n_all):
    return jnp.where(s < n_ctx, n_ctx - 1 - s, n_all + n_ctx - 1 - s)


def _scan_chunk_index(d, s, n_ctx, n_all):
    return jnp.where(d == 0, s, _bwd_chunk_index(s, n_ctx, n_all))


def _rwkv_scan(r, kk, v, lw, a, kd):
    bsz, n, _ = r.shape
    c = RWKV_CHUNK
    n_all, n_ctx = n // c, CTX_LEN // c
    blk = (bsz, c, GROUP_W)
    fwd = [pl.BlockSpec(blk, lambda s: (0, s, 0))] * 6
    bwd = ([pl.BlockSpec(blk, lambda s: (0, _bwd_chunk_index(s, n_ctx, n_all), 0))] * 3
           + [pl.BlockSpec(blk, lambda s: (0, _bwd_chunk_index(s, n_ctx, n_all), 1))] * 3)
    return pl.pallas_call(
        functools.partial(_rwkv_scan_kernel, bsz=bsz),
        grid=(n_all,),
        in_specs=fwd + bwd,
        out_specs=[pl.BlockSpec(blk, lambda s: (0, s, 0)),
                   pl.BlockSpec(blk, lambda s: (0, _bwd_chunk_index(s, n_ctx, n_all), 0))],
        out_shape=[jax.ShapeDtypeStruct((bsz, n, GROUP_W), F32)] * 2,
        scratch_shapes=[pltpu.VMEM((2 * bsz * (GROUP_W // LANES), LANES, LANES), F32)],
        compiler_params=pltpu.CompilerParams(dimension_semantics=("arbitrary",), vmem_limit_bytes=VMEM_LIMIT),
        name="rwkv_scan",
    )(r, kk, v, lw, a, kd, r, kk, v, lw, a, kd)


def _ssd_prep_kernel(p_ref, prev_ref, next_ref, cw_ref, cb_ref, dtb_ref, aneg_ref, e_ref, pa_ref,
                     xs_ref, bc_ref, xdt_ref, adt_ref):
    i = pl.program_id(1)
    has_prev, has_next = _halo_valid(i, pl.num_programs(1))
    x = p_ref[0, :, 512:1536]
    prev = jnp.where(has_prev, prev_ref[0, :, 512:1536], 0.0)
    nxt = jnp.where(has_next, next_ref[0, :, 512:1536], 0.0)
    taps = _shifted_rows(x, prev, nxt, (2, 1, 0, -1, -2))
    conv = cb_ref[...]
    for j, tap in enumerate(taps):
        conv = conv + cw_ref[j:j + 1, :] * tap
    xbc = _silu(conv)
    xs = xbc[:, 0:512]
    xs_ref[0] = xs
    bc_ref[0] = xbc[:, 512:1024]
    t = p_ref[0, :, 1536:1664] + dtb_ref[...]
    dt = jnp.maximum(t, 0.0) + jnp.log(1.0 + jnp.exp(-jnp.abs(t)))
    xdt_ref[0] = jnp.concatenate([xs, xs], axis=1) * _dot_exact_rhs(dt, e_ref[...])
    adt_ref[0] = _dot_exact_rhs(dt * aneg_ref[...], pa_ref[...])


def _ssd_prep(p, consts):
    bsz, n, _ = p.shape
    tm = ROW_TILE
    n_t = n // tm
    per = tm // HALO
    row = lambda b, i: (b, i, 0)
    const = lambda b, i: (0, 0)
    in_specs = [pl.BlockSpec((1, tm, P_SSD), row),
                pl.BlockSpec((1, HALO, P_SSD), lambda b, i: (b, jnp.maximum(i * per - 1, 0), 0)),
                pl.BlockSpec((1, HALO, P_SSD), lambda b, i: (b, jnp.minimum((i + 1) * per, n_t * per - 1), 0))]
    in_specs += [pl.BlockSpec(c.shape, const) for c in consts]
    widths = (512, 512, 1024, 256)
    return pl.pallas_call(
        _ssd_prep_kernel,
        grid=(bsz, n_t),
        in_specs=in_specs,
        out_specs=[pl.BlockSpec((1, tm, w), row) for w in widths],
        out_shape=[jax.ShapeDtypeStruct((bsz, n, w), F32) for w in widths],
        compiler_params=pltpu.CompilerParams(dimension_semantics=("parallel", "parallel"),
                                             vmem_limit_bytes=VMEM_LIMIT),
        name="ssd_prep",
    )(p, p, p, *consts)


def _ssd_scan_kernel(x_ref, a_ref, bc_ref, y_ref, ht_sc):
    q = SSD_CHUNK
    fwd = pl.program_id(0) == 0

    @pl.when(pl.program_id(2) == 0)
    def _():
        ht_sc[...] = jnp.zeros_like(ht_sc)

    ti = lax.broadcasted_iota(jnp.int32, (q, q), 0)
    si = lax.broadcasted_iota(jnp.int32, (q, q), 1)
    tri = jnp.where(fwd, ti - si, si - ti) >= 0
    acum = _dot_exact_lhs(jnp.where(tri, 1.0, 0.0).astype(BF16), a_ref[0])
    acum_t = acum.T
    last = jnp.where(fwd, acum[q - 1:q, :], acum[0:1, :])
    m0 = lax.broadcasted_iota(jnp.int32, (1, LANES), 1) < HEAD_DIM
    bc = bc_ref[0]

    def decay(h):
        seg = acum[:, h:h + 1] - acum_t[h:h + 1, :]
        return jnp.where(tri, jnp.exp(jnp.where(tri, seg, 0.0)), 0.0)

    def per_head(fn, h0):
        return jnp.where(m0, fn(h0), fn(h0 + 1))

    for g in range(2):
        bg = bc[:, g * SSD_STATE:(g + 1) * SSD_STATE]
        cg = bc[:, 256 + g * SSD_STATE:256 + (g + 1) * SSD_STATE]
        cg_b = _bf(cg)
        scores = _dot_nt(cg_b, _bf(bg))
        bg_t = _bf(bg.T)
        for pp in range(2):
            pair = 2 * g + pp
            h0 = 2 * pair
            sl = slice(pair * LANES, (pair + 1) * LANES)
            xp = x_ref[0, :, sl]
            xp_b = _bf(xp)
            y0 = _dot(_bf(scores * decay(h0)), xp_b)
            y1 = _dot(_bf(scores * decay(h0 + 1)), xp_b)
            ht = ht_sc[pair]
            carry = _dot(cg_b, _bf(ht)) * per_head(lambda h: jnp.exp(acum[:, h:h + 1]), h0)
            y_ref[0, 0, :, sl] = jnp.where(m0, y0, y1) + carry
            to_end = per_head(lambda h: jnp.exp(last[:, h:h + 1] - acum[:, h:h + 1]), h0)
            e_last = per_head(lambda h: jnp.exp(last[:, h:h + 1]), h0)
            ht_sc[pair] = ht * e_last + _dot(bg_t, _bf(xp * to_end))


def _ssd_scan(xdt, adt, bc):
    bsz, n, _ = bc.shape
    q = SSD_CHUNK
    n_all, n_ctx = n // q, CTX_LEN // q
    idx = lambda d, s: _scan_chunk_index(d, s, n_ctx, n_all)
    return pl.pallas_call(
        _ssd_scan_kernel,
        grid=(2, bsz, n_all),
        in_specs=[pl.BlockSpec((1, q, GROUP_W), lambda d, b, s: (b, idx(d, s), d)),
                  pl.BlockSpec((1, q, LANES), lambda d, b, s: (b, idx(d, s), d)),
                  pl.BlockSpec((1, q, GROUP_W), lambda d, b, s: (b, idx(d, s), 0))],
        out_specs=pl.BlockSpec((1, 1, q, GROUP_W), lambda d, b, s: (d, b, idx(d, s), 0)),
        out_shape=jax.ShapeDtypeStruct((2, bsz, n, GROUP_W), F32),
        scratch_shapes=[pltpu.VMEM((GROUP_W // LANES, SSD_STATE, LANES), F32)],
        compiler_params=pltpu.CompilerParams(dimension_semantics=("parallel", "parallel", "arbitrary"),
                                             vmem_limit_bytes=VMEM_LIMIT),
        name="ssd_scan",
    )(xdt, adt, bc)


def _post_kernel(z_ref, mctx_ref, mlat_ref, od_ref, om_ref, yrf_ref, yrb_ref, r_ref, k_ref, v_ref, g_ref,
                 lng_ref, lnb_ref, rk_ref, ysf_ref, ysb_ref, xs_ref, zg_ref, dsk_ref, ng_ref, smean_ref, ssum_ref,
                 w_ref, o_ref, *, tile_off):
    is_ctx = _is_ctx_rows(pl.program_id(1) + tile_off, ROW_TILE, CTX_LEN)
    smean = smean_ref[...]
    y = yrf_ref[0] + yrb_ref[0]
    yc = y - _group_stat(y, smean)
    yn = yc * lax.rsqrt(_group_stat(yc * yc, smean) + RWKV_LN_EPS) * lng_ref[...] + lnb_ref[...]
    bonus = _group_stat(r_ref[0] * k_ref[0] * rk_ref[...], ssum_ref[...]) * v_ref[0]
    o_rwkv = (yn + bonus) * g_ref[0]

    t = (ysf_ref[0, 0] + ysb_ref[0, 0] + dsk_ref[...] * xs_ref[0]) * _silu(zg_ref[0])
    o_ssd = t * lax.rsqrt(jnp.mean(t * t, axis=-1, keepdims=True) + 1e-6) * ng_ref[...]

    mix = (_dot(od_ref[0], w_ref[0:512, :]) + _dot(_bf(o_rwkv), w_ref[512:1024, :])
           + _dot(om_ref[0], w_ref[1024:1536, :]) + _dot(_bf(o_ssd), w_ref[1536:2048, :]))
    o_ref[0] = z_ref[0] + _mod_row(mctx_ref, mlat_ref, 5, is_ctx) * mix


def _post(z, m, od, om, yr, r, k, v, g, ys, xs, p_ssd, consts, w, *, want_ctx):
    bsz, n, d = z.shape
    tm = ROW_TILE
    off = 0 if want_ctx else 1
    n_out = n - off * tm
    full = lambda b, i: (b, i + off, 0)
    attn = lambda b, i: (b, i, 0)
    const = lambda b, i: (0, 0)
    seq = lambda width: pl.BlockSpec((1, tm, width), full)
    dirs = lambda dd: pl.BlockSpec((1, 1, tm, GROUP_W), lambda b, i: (dd, b, i + off, 0))
    in_specs = [seq(d),
                pl.BlockSpec((1, N_MOD, d), lambda b, i: (0, 0, 0)),
                pl.BlockSpec((1, N_MOD, d), lambda b, i: (b + 1, 0, 0)),
                pl.BlockSpec((1, tm, GROUP_W), attn), pl.BlockSpec((1, tm, GROUP_W), attn),
                seq(GROUP_W), seq(GROUP_W), seq(GROUP_W), seq(GROUP_W), seq(GROUP_W), seq(GROUP_W)]
    in_specs += [pl.BlockSpec(c.shape, const) for c in consts[:3]]
    in_specs += [dirs(0), dirs(1), seq(GROUP_W), seq(GROUP_W)]
    in_specs += [pl.BlockSpec(c.shape, const) for c in consts[3:]]
    in_specs += [pl.BlockSpec(w.shape, const, pipeline_mode=pl.Buffered(1))]
    return pl.pallas_call(
        functools.partial(_post_kernel, tile_off=off),
        grid=(bsz, n_out // tm),
        in_specs=in_specs,
        out_specs=pl.BlockSpec((1, tm, d), lambda b, i: (b, i, 0)),
        out_shape=jax.ShapeDtypeStruct((bsz, n_out, d), F32),
        compiler_params=pltpu.CompilerParams(dimension_semantics=("parallel", "parallel"),
                                             vmem_limit_bytes=VMEM_LIMIT),
        name="post",
    )(z, m, m, od, om, yr[0], yr[1], r, k, v, g, *consts[:3], ys, ys, xs, p_ssd, *consts[3:], w)


def _pad_cols(w, width):
    return jnp.pad(w, ((0, 0), (0, width - w.shape[1])))


def _block_diag_const(block, reps):
    return jnp.asarray(np.kron(np.eye(reps, dtype=np.float32), block), BF16)


def _group_consts():
    ones64 = np.ones((64, 64), np.float32)
    mla_blk = np.zeros((LANES, LANES), np.float32)
    mla_blk[:MLA_NOPE, :MLA_NOPE] = 1.0 / MLA_NOPE
    mla_blk[MLA_NOPE:MLA_NOPE + MLA_ROPE, MLA_NOPE:MLA_NOPE + MLA_ROPE] = 1.0 / MLA_ROPE
    return (_block_diag_const(ones64 / 64.0, 4), _block_diag_const(ones64, 4), _block_diag_const(mla_blk, 2))


def _rope_angles(n_lat, dim):
    quarter = dim // 4
    inv = ROPE_BASE ** (-jnp.arange(quarter, dtype=F32) / quarter)
    rows = n_lat // GRID_W
    pos_r = jnp.repeat(jnp.arange(rows), GRID_W).astype(F32)
    pos_c = jnp.tile(jnp.arange(GRID_W), rows).astype(F32)
    ang = jnp.concatenate([pos_r[:, None] * inv, pos_c[:, None] * inv], axis=-1)
    return jnp.cos(ang), jnp.sin(ang)


def _rope_lane_tables(n_lat, dim, lane_off):
    cos, sin = _rope_angles(n_lat, dim)
    cos_t = jnp.ones((n_lat, LANES), F32).at[:, lane_off:lane_off + dim].set(jnp.concatenate([cos, cos], axis=-1))
    sin_t = jnp.zeros((n_lat, LANES), F32).at[:, lane_off:lane_off + dim].set(jnp.concatenate([-sin, sin], axis=-1))
    cos_t = jnp.concatenate([jnp.ones((CTX_LEN, LANES), F32), cos_t], axis=0)
    sin_t = jnp.concatenate([jnp.zeros((CTX_LEN, LANES), F32), sin_t], axis=0)
    return cos_t, sin_t


def _row(x):
    return x.reshape(1, -1).astype(F32)


def _mla_consts(q_norm_g, kv_norm_g, w_uq, w_ukv, nope_g, rope_g, s_mla):
    hq = MLA_NOPE + MLA_ROPE
    w_uq = w_uq.reshape(MLA_Q_RANK, MLA_HEADS, hq)
    w_uq = jnp.pad(w_uq, ((0, 0), (0, 0), (0, LANES - hq))).reshape(MLA_Q_RANK, MLA_HEADS * LANES)
    w_ukv = w_ukv.reshape(MLA_KV_RANK, MLA_HEADS, 2 * MLA_NOPE)
    w_k = jnp.pad(w_ukv[:, :, :MLA_NOPE], ((0, 0), (0, 0), (0, LANES - MLA_NOPE))).reshape(MLA_KV_RANK, -1)
    w_v = w_ukv[:, :, MLA_NOPE:].reshape(MLA_KV_RANK, GROUP_W)
    zpad = jnp.zeros((LANES - hq,), F32)
    g_q = jnp.tile(jnp.concatenate([nope_g[0], rope_g[0], zpad]), MLA_HEADS)
    g_k = jnp.tile(jnp.concatenate([nope_g[1], jnp.zeros((LANES - MLA_NOPE,), F32)]), MLA_HEADS)
    g_kr = jnp.concatenate([rope_g[1], jnp.zeros((LANES - MLA_ROPE,), F32)])
    place = np.zeros((LANES, MLA_HEADS * LANES), np.float32)
    for h in range(MLA_HEADS):
        place[np.arange(MLA_ROPE), h * LANES + MLA_NOPE + np.arange(MLA_ROPE)] = 1.0
    return (_row(q_norm_g), _row(kv_norm_g), _bf(w_uq), _bf(w_k), _bf(w_v), _row(g_q), _row(g_k), _row(g_kr),
            jnp.asarray(place, BF16), s_mla)


def _two_dir_block_diag(w):
    z = jnp.zeros_like(w[0])
    return jnp.concatenate([jnp.concatenate([w[0], z], axis=1), jnp.concatenate([z, w[1]], axis=1)], axis=0)


def _ssd_expand_consts():
    e = np.zeros((LANES, 2 * GROUP_W), np.float32)
    pa = np.zeros((LANES, 2 * LANES), np.float32)
    for d in range(2):
        for h in range(SSD_HEADS):
            e[d * SSD_HEADS + h, d * GROUP_W + h * HEAD_DIM:d * GROUP_W + (h + 1) * HEAD_DIM] = 1.0
            pa[d * SSD_HEADS + h, d * LANES + h] = 1.0
    return jnp.asarray(e, BF16), jnp.asarray(pa, BF16)


def _pad_lanes(v):
    return jnp.pad(v.reshape(1, -1).astype(F32), ((0, 0), (0, LANES - v.size)))


def kernel(x, c, ctx, c_ctx, mod_w, mod_b, norm_g, ffn_w_gate, ffn_w_up, ffn_w_down, w_in, w_out, diff_qk_g, diff_lambda, diff_subln_g, rwkv_shift_w, rwkv_w0, rwkv_w_up, rwkv_a0, rwkv_a_up, rwkv_g_up, rwkv_k_k, rwkv_k_a, rwkv_r_k, rwkv_ln_g, rwkv_ln_b, mla_q_norm_g, mla_kv_norm_g, mla_w_uq, mla_w_ukv, mla_nope_g, mla_rope_g, ssd_conv_w, ssd_conv_b, ssd_dt_bias, ssd_a_log, ssd_d, ssd_norm_g):
    bsz, n_lat, d = x.shape
    depth = mod_w.shape[0]
    s_mean64, s_sum64, s_mla = _group_consts()
    e_dt, p_adt = _ssd_expand_consts()
    cos_d, sin_d = _rope_lane_tables(n_lat, HEAD_DIM, 0)
    cos_d = cos_d.at[:, HEAD_DIM:].set(cos_d[:, :HEAD_DIM])
    sin_d = sin_d.at[:, HEAD_DIM:].set(sin_d[:, :HEAD_DIM])
    cos_ma, sin_ma = _rope_lane_tables(n_lat, MLA_ROPE, MLA_NOPE)
    cos_mb, sin_mb = _rope_lane_tables(n_lat, MLA_ROPE, 0)

    cvec = jnp.zeros((8, d), F32).at[0].set(c_ctx).at[1:1 + bsz].set(c)
    z = jnp.concatenate([ctx, x], axis=1)
    tm_ffn = 640

    for l in range(depth):
        want_ctx = l < depth - 1
        lam_init = 0.8 - 0.6 * float(np.exp(-0.3 * l))
        m = _mod(cvec, mod_w[l], mod_b[l].reshape(1, N_MOD * d)).reshape(8, N_MOD, d)

        z = _ffn(z, m, _row(norm_g[l, 0]), _bf(ffn_w_gate[l, 0]), _bf(ffn_w_up[l, 0]), _bf(ffn_w_down[l, 0]),
                 j=0, ctx_len=CTX_LEN, tm=tm_ffn)

        w = w_in[l]
        w_p = jnp.concatenate([w[:, :3456], _pad_cols(w[:, 3456:4000], P_MLA), _pad_cols(w[:, 4000:], P_SSD)], axis=1)
        p_diff, p_rwkv, p_mla, p_ssd = _inproj(z, m, _row(norm_g[l, 1]), _bf(w_p))

        qd, kd, vd = _diff_prep(p_diff, jnp.tile(diff_qk_g[l], (1, 2 * DIFF_HEADS)), cos_d, sin_d, s_mean64)
        o_diff = _attention(qd, kd, vd, (diff_lambda[l], diff_subln_g[l].reshape(-1, 1)), mode="diff", want_ctx=want_ctx,
                            lam_init=lam_init)

        qm, km, vm = _mla_prep(p_mla, _mla_consts(mla_q_norm_g[l], mla_kv_norm_g[l], mla_w_uq[l], mla_w_ukv[l],
                                                  mla_nope_g[l], mla_rope_g[l], s_mla),
                               (cos_ma, sin_ma, cos_mb, sin_mb))
        o_mla = _attention(qm, km, vm, (), mode="mla", want_ctx=want_ctx)

        rw_consts = (rwkv_shift_w[l], _row(rwkv_w0[l]), _bf(_two_dir_block_diag(rwkv_w_up[l])), _row(rwkv_a0[l]),
                     _bf(_two_dir_block_diag(rwkv_a_up[l])), _bf(rwkv_g_up[l]), _row(rwkv_k_k[l]),
                     _row(rwkv_k_a[l]), s_sum64)
        r, k, v, kk, lw, a, kdir, g = _rwkv_prep(p_rwkv, rw_consts)
        y_rwkv = _rwkv_scan(r, kk, v, lw, a, kdir)

        sd_consts = (ssd_conv_w[l], _row(ssd_conv_b[l]), _pad_lanes(ssd_dt_bias[l]),
                     _pad_lanes(-jnp.exp(ssd_a_log[l])), e_dt, p_adt)
        xs, bc, xdt, adt = _ssd_prep(p_ssd, sd_consts)
        y_ssd = _ssd_scan(xdt, adt, bc)

        post_consts = (_row(rwkv_ln_g[l]), _row(rwkv_ln_b[l]), _row(rwkv_r_k[l]),
                       _row(jnp.repeat(ssd_d[l], HEAD_DIM)), _row(ssd_norm_g[l]), s_mean64, s_sum64)
        z = _post(z, m, o_diff, o_mla, y_rwkv, r, k, v, g, y_ssd, xs, p_ssd, post_consts, _bf(w_out[l]),
                  want_ctx=want_ctx)

        z = _ffn(z, m, _row(norm_g[l, 2]), _bf(ffn_w_gate[l, 1]), _bf(ffn_w_up[l, 1]), _bf(ffn_w_down[l, 1]),
                 j=2, ctx_len=CTX_LEN if want_ctx else 0, tm=tm_ffn if want_ctx else 512)
    return z
```

```python
import functools

import jax
import jax.numpy as jnp
import numpy as np
from jax import lax
from jax.experimental import pallas as pl
from jax.experimental.pallas import tpu as pltpu

F32 = jnp.float32
BF16 = jnp.bfloat16

CTX_LEN = 256
GRID_W = 64
HEAD_DIM = 64
ROPE_BASE = 10000.0
N_MOD = 9
GROUP_W = 512
DIFF_HEADS = 4
MLA_HEADS = 8
MLA_NOPE = 64
MLA_ROPE = 32
MLA_Q_RANK = 384
MLA_KV_RANK = 128
RWKV_DECAY_SCALE = 0.606531
RWKV_LN_EPS = 64e-5
RWKV_CHUNK = 64
SSD_HEADS = 8
SSD_STATE = 128
SSD_CHUNK = 128
LOG2E = 1.4426950408889634

LANES = 128
ROW_TILE = CTX_LEN
ADALN_ROWS = 16
HALO = 8
ATTN_KB = ROW_TILE
ATTN_NB = 5
ONES_ROWS = 16
VMEM_LIMIT = 56 * 1024 * 1024

P_DIFF = 1536
P_RWKV = 1920
P_MLA = 640
P_SSD = 1664
P_OFFS = (0, P_DIFF, P_DIFF + P_RWKV, P_DIFF + P_RWKV + P_MLA, P_DIFF + P_RWKV + P_MLA + P_SSD)


def _dot(a, b):
    return jnp.dot(a, b, preferred_element_type=F32)


def _dot_nt(a, b):
    return lax.dot_general(a, b, (((1,), (1,)), ((), ())), preferred_element_type=F32)


def _bf(x):
    return x.astype(BF16)


def _split3(x):
    x1 = x.astype(BF16)
    r1 = x - x1.astype(F32)
    x2 = r1.astype(BF16)
    r2 = r1 - x2.astype(F32)
    return x1, x2, r2.astype(BF16)


def _dot_exact_lhs(a, x):
    x1, x2, x3 = _split3(x)
    return _dot(a, x1) + _dot(a, x2) + _dot(a, x3)


def _dot_exact_rhs(x, a):
    x1, x2, x3 = _split3(x)
    return _dot(x1, a) + _dot(x2, a) + _dot(x3, a)


def _group_stat(x, s):
    hi = x.astype(BF16)
    lo = (x - hi.astype(F32)).astype(BF16)
    blk = s.shape[0]
    outs = []
    for c in range(x.shape[-1] // blk):
        sl = slice(c * blk, (c + 1) * blk)
        outs.append(_dot(hi[:, sl], s) + _dot(lo[:, sl], s))
    return outs[0] if len(outs) == 1 else jnp.concatenate(outs, axis=-1)


def _swap_halves(x, half):
    n = x.shape[-1]
    lane = lax.broadcasted_iota(jnp.int32, (1, n), 1)
    up = pltpu.roll(x, n - half, axis=1)
    dn = pltpu.roll(x, half, axis=1)
    return jnp.where((lane & (2 * half - 1)) < half, up, dn)


def _silu(x):
    return x * jax.nn.sigmoid(x)


def _is_ctx_rows(tile_idx, tm, ctx_len):
    rows = tile_idx * tm + lax.broadcasted_iota(jnp.int32, (tm, 1), 0)
    return rows < ctx_len


def _mod_row(mctx_ref, mlat_ref, idx, is_ctx):
    return jnp.where(is_ctx, mctx_ref[0, idx:idx + 1, :], mlat_ref[0, idx:idx + 1, :])


def _adaln(z, g, mctx_ref, mlat_ref, j, is_ctx):
    ms = jnp.mean(z * z, axis=-1, keepdims=True)
    h = z * lax.rsqrt(ms + 1e-6) * g
    return h * (1.0 + _mod_row(mctx_ref, mlat_ref, 3 * j + 1, is_ctx)) + _mod_row(mctx_ref, mlat_ref, 3 * j, is_ctx)


def _adaln_to_scratch(z_ref, g_ref, mctx_ref, mlat_ref, j, tile_idx, tm, ctx_len, h_sc):
    assert ctx_len % ADALN_ROWS == 0 and tm % ADALN_ROWS == 0
    g = g_ref[...]

    def rows(r, carry):
        sl = pl.ds(pl.multiple_of(r * ADALN_ROWS, ADALN_ROWS), ADALN_ROWS)
        step_is_ctx = tile_idx * tm + r * ADALN_ROWS < ctx_len
        h_sc[sl, :] = _bf(_adaln(z_ref[0, sl, :], g, mctx_ref, mlat_ref, j, step_is_ctx))
        return carry

    lax.fori_loop(0, tm // ADALN_ROWS, rows, 0, unroll=4)


def _shifted_rows(x, prev, nxt, shifts):
    tm = x.shape[0]
    ext = jnp.concatenate([prev, x, nxt], axis=0)
    n = tm + 2 * HALO
    out = []
    for s in shifts:
        out.append(x if s == 0 else pltpu.roll(ext, s % n, axis=0)[HALO:HALO + tm])
    return out


def _mod_kernel(c_ref, w_ref, b_ref, o_ref):
    o_ref[...] = _dot(_bf(_silu(c_ref[...])), _bf(w_ref[...])) + b_ref[...]


def _mod(cvec, w, b, *, layer):
    _, d, n = w.shape
    tn = 1024
    return pl.pallas_call(
        _mod_kernel,
        grid=(n // tn,),
        in_specs=[pl.BlockSpec((8, d), lambda i: (0, 0)),
                  pl.BlockSpec((None, d, tn), lambda i: (layer, 0, i)),
                  pl.BlockSpec((None, 1, tn), lambda i: (layer, 0, i))],
        out_specs=pl.BlockSpec((8, tn), lambda i: (0, i)),
        out_shape=jax.ShapeDtypeStruct((8, n), F32),
        compiler_params=pltpu.CompilerParams(dimension_semantics=("parallel",), vmem_limit_bytes=VMEM_LIMIT),
        name="mod",
    )(cvec, w, b)


def _ffn_kernel(z_ref, mctx_ref, mlat_ref, g_ref, wg_ref, wu_ref, wd_ref, o_ref, h_sc, acc_sc, *, j, ctx_len, tm):
    f = pl.program_id(2)

    @pl.when(f == 0)
    def _():
        _adaln_to_scratch(z_ref, g_ref, mctx_ref, mlat_ref, j, pl.program_id(1), tm, ctx_len, h_sc)
        acc_sc[...] = jnp.zeros_like(acc_sc)

    h = h_sc[...]
    act = _silu(_dot(h, wg_ref[...])) * _dot(h, wu_ref[...])
    acc_sc[...] += _dot(_bf(act), wd_ref[...])

    @pl.when(f == pl.num_programs(2) - 1)
    def _():
        gate = _mod_row(mctx_ref, mlat_ref, 3 * j + 2, _is_ctx_rows(pl.program_id(1), tm, ctx_len))
        o_ref[0] = z_ref[0] + gate * (0.5 * acc_sc[...])


def _ffn(z, m, g, wg, wu, wd, *, layer, which, j, ctx_len, tm):
    bsz, n, d = z.shape
    ff = wg.shape[-1]
    tf = 512
    return pl.pallas_call(
        functools.partial(_ffn_kernel, j=j, ctx_len=ctx_len, tm=tm),
        grid=(bsz, n // tm, ff // tf),
        in_specs=[pl.BlockSpec((1, tm, d), lambda b, i, f: (b, i, 0)),
                  pl.BlockSpec((1, N_MOD, d), lambda b, i, f: (0, 0, 0)),
                  pl.BlockSpec((1, N_MOD, d), lambda b, i, f: (b + 1, 0, 0)),
                  pl.BlockSpec((1, d), lambda b, i, f: (0, 0)),
                  pl.BlockSpec((None, None, d, tf), lambda b, i, f: (layer, which, 0, f)),
                  pl.BlockSpec((None, None, d, tf), lambda b, i, f: (layer, which, 0, f)),
                  pl.BlockSpec((None, None, tf, d), lambda b, i, f: (layer, which, f, 0))],
        out_specs=pl.BlockSpec((1, tm, d), lambda b, i, f: (b, i, 0)),
        out_shape=jax.ShapeDtypeStruct(z.shape, F32),
        scratch_shapes=[pltpu.VMEM((tm, d), BF16), pltpu.VMEM((tm, d), F32)],
        compiler_params=pltpu.CompilerParams(dimension_semantics=("parallel", "parallel", "arbitrary"),
                                             vmem_limit_bytes=VMEM_LIMIT),
        name="ffn",
    )(z, m, m, g, wg, wu, wd)


N_DIFF_IN = 4
N_MLA_IN = 14


def _inproj_kernel(*refs):
    (z_ref, mctx_ref, mlat_ref, g_ref, w_ref), refs = refs[:5], refs[5:]
    diff_in, refs = refs[:N_DIFF_IN], refs[N_DIFF_IN:]
    mla_in, refs = refs[:N_MLA_IN], refs[N_MLA_IN:]
    qd_ref, kd_ref, vtd_ref, or_ref, qm_ref, km_ref, vtm_ref, os_ref = refs
    is_ctx = pl.program_id(1) * ROW_TILE < CTX_LEN
    h = _bf(_adaln(z_ref[0], g_ref[...], mctx_ref, mlat_ref, 1, is_ctx))

    def proj(group):
        return _dot(h, w_ref[:, P_OFFS[group]:P_OFFS[group + 1]])

    _diff_prep(proj(0), *diff_in, qd_ref, kd_ref, vtd_ref)
    or_ref[0] = proj(1)
    _mla_prep(proj(2), *mla_in, qm_ref, km_ref, vtm_ref)
    os_ref[0] = proj(3)


def _inproj(z, m, g, w, diff_consts, diff_tabs, mla_consts, mla_tabs):
    bsz, n, d = z.shape
    tm = ROW_TILE
    row = lambda b, i: (b, i, 0)
    const = lambda b, i: (0, 0)
    tab = pl.BlockSpec((tm, LANES), lambda b, i: (i, 0))
    mla_w = MLA_HEADS * LANES
    in_specs = [pl.BlockSpec((1, tm, d), row),
                pl.BlockSpec((1, N_MOD, d), lambda b, i: (0, 0, 0)),
                pl.BlockSpec((1, N_MOD, d), lambda b, i: (b + 1, 0, 0)),
                pl.BlockSpec((1, d), const),
                pl.BlockSpec(w.shape, const, pipeline_mode=pl.Buffered(1))]
    in_specs += [pl.BlockSpec(diff_consts[0].shape, const), tab, tab, pl.BlockSpec(diff_consts[1].shape, const)]
    in_specs += [pl.BlockSpec(c.shape, const) for c in mla_consts] + [tab] * 4
    seq = lambda width: pl.BlockSpec((1, tm, width), row)
    sds = lambda width, dt: jax.ShapeDtypeStruct((bsz, n, width), dt)
    return pl.pallas_call(
        _inproj_kernel,
        grid=(bsz, n // tm),
        in_specs=in_specs,
        out_specs=[seq(GROUP_W), seq(GROUP_W), _VT_SPEC, seq(P_RWKV), seq(mla_w), seq(mla_w), _VT_SPEC, seq(P_SSD)],
        out_shape=[sds(GROUP_W, BF16), sds(GROUP_W, BF16), _vt_shape(bsz, n), sds(P_RWKV, F32),
                   sds(mla_w, BF16), sds(mla_w, BF16), _vt_shape(bsz, n), sds(P_SSD, F32)],
        compiler_params=pltpu.CompilerParams(dimension_semantics=("parallel", "parallel"),
                                             vmem_limit_bytes=VMEM_LIMIT),
        name="inproj",
    )(z, m, m, g, w, diff_consts[0], *diff_tabs, diff_consts[1], *mla_consts, *mla_tabs)


_VT_SPEC = pl.BlockSpec((1, 1, GROUP_W, ROW_TILE), lambda b, i: (b, i, 0, 0))


def _vt_shape(bsz, n):
    return jax.ShapeDtypeStruct((bsz, n // ROW_TILE, GROUP_W, ROW_TILE), BF16)


def _diff_prep(p, g_ref, cos_ref, sin_ref, s_ref, q_ref, k_ref, vt_ref):
    s = s_ref[...]
    cos = jnp.concatenate([cos_ref[...]] * 4, axis=1)
    sin = jnp.concatenate([sin_ref[...]] * 4, axis=1)

    def norm_rope(x, g):
        xn = x * lax.rsqrt(_group_stat(x * x, s) + 1e-6) * g
        return xn * cos + _swap_halves(xn, HEAD_DIM // 2) * sin

    q = norm_rope(p[:, 0:GROUP_W], g_ref[0:1, :]) * (HEAD_DIM ** -0.5 * LOG2E)
    k = norm_rope(p[:, GROUP_W:2 * GROUP_W], g_ref[1:2, :])
    q_ref[0] = _bf(q)
    k_ref[0] = _bf(k)
    vt_ref[0, 0] = _bf(p[:, 2 * GROUP_W:3 * GROUP_W].T)


def _mla_prep(p, qng_ref, kvng_ref, wuq_ref, wk_ref, wv_ref, gq_ref, gk_ref, gkr_ref, place_ref, s_ref,
              cosa_ref, sina_ref, cosb_ref, sinb_ref, q_ref, k_ref, vt_ref):
    s = s_ref[...]
    half = MLA_ROPE // 2

    def rms(x, width):
        return x * lax.rsqrt(jnp.sum(x * x, axis=-1, keepdims=True) * (1.0 / width) + 1e-6)

    cqn = _bf(rms(p[:, 0:MLA_Q_RANK], MLA_Q_RANK) * qng_ref[...])
    q = _dot(cqn, wuq_ref[...])
    cosa = jnp.concatenate([cosa_ref[...]] * MLA_HEADS, axis=1)
    sina = jnp.concatenate([sina_ref[...]] * MLA_HEADS, axis=1)
    qn = q * lax.rsqrt(_group_stat(q * q, s) + 1e-6) * gq_ref[...]
    qr = (qn * cosa + _swap_halves(qn, half) * sina) * ((MLA_NOPE + MLA_ROPE) ** -0.5 * LOG2E)
    q_ref[0] = _bf(qr)

    ckvn = _bf(rms(p[:, MLA_Q_RANK:MLA_Q_RANK + MLA_KV_RANK], MLA_KV_RANK) * kvng_ref[...])
    kn = _dot(ckvn, wk_ref[...])
    knn = kn * lax.rsqrt(_group_stat(kn * kn, s) + 1e-6) * gk_ref[...]
    vt_ref[0, 0] = _bf(_dot(ckvn, wv_ref[...]).T)

    k_rope_lo = MLA_Q_RANK + MLA_KV_RANK
    krn = rms(p[:, k_rope_lo:k_rope_lo + LANES], MLA_ROPE) * gkr_ref[...]
    krr = krn * cosb_ref[...] + _swap_halves(krn, half) * sinb_ref[...]
    k_ref[0] = _bf(knn + _dot(_bf(krr), place_ref[...]))


def _attn_kernel(*refs, mode, n_keys, first_is_ctx, lam_init):
    if mode == "diff":
        q_ref, k_ref, vt_ref, lam_ref, g_ref, o_ref, s_sc = refs
    else:
        q_ref, k_ref, vt_ref, o_ref, s_sc = refs
    tq = q_ref.shape[1]
    qt = _bf(q_ref[0].astype(F32).T)
    row_lo = lax.broadcasted_iota(jnp.int32, (LANES, 1), 0) < HEAD_DIM
    if mode == "diff":
        zero = jnp.zeros_like(qt)
        qa, qb = jnp.where(row_lo, qt, zero), jnp.where(row_lo, zero, qt)
        v_dim, v_rows = LANES, (slice(None), slice(None))
    else:
        qa, qb = qt[:LANES], qt[LANES:]
        v_dim, v_rows = HEAD_DIM, (slice(0, HEAD_DIM), slice(HEAD_DIM, LANES))

    def sweep(n_iter, nb):
        def scores(c, slot):
            out = []
            rows = nb * ATTN_KB
            kc = k_ref[0, pl.ds(pl.multiple_of(c * rows, rows), rows), :]
            for half, qm in enumerate((qa, qb)):
                s = _dot(kc if mode == "diff" else kc[:, half * LANES:(half + 1) * LANES], qm)
                s_sc[slot, half, 0:rows, :] = s
                out.append(jnp.max(s, axis=0, keepdims=True))
            return tuple(out)

        def accumulate(c, slot, mxs, state):
            new = []
            for half, (mx, (m, acc)) in enumerate(zip(mxs, state)):
                m_new = jnp.maximum(m, mx)
                p = jnp.exp2(s_sc[slot, half, 0:nb * ATTN_KB, :] - m_new)
                vg = jnp.concatenate([vt_ref[0, c * nb + j, v_rows[half], :] for j in range(nb)], axis=1)
                vg = jnp.concatenate([vg, jnp.ones((ONES_ROWS, nb * ATTN_KB), BF16)], axis=0)
                acc = jnp.exp2(m - m_new) * acc + _dot(vg, _bf(p))
                new.append((m_new, acc))
            return tuple(new)

        def body(c2, carry):
            mxs, state = carry
            c = 2 * c2
            mxs1 = scores(c + 1, 1)
            state = accumulate(c, 0, mxs, state)
            mxs0 = scores(c + 2, 0)
            state = accumulate(c + 1, 1, mxs1, state)
            return mxs0, state

        init = (jnp.full((1, tq), -1e30, F32), jnp.zeros((v_dim + ONES_ROWS, tq), F32))
        n_pairs = (n_iter - 1) // 2
        mxs, state = lax.fori_loop(0, n_pairs, body, (scores(0, 0), (init, init)),
                                   unroll=2 if n_pairs % 2 == 0 and n_pairs else 1)
        (_, acca), (_, accb) = accumulate(n_iter - 1, 0, mxs, state)
        return acca, accb


    def finish(carry):
        acca, accb = carry
        oa = acca[:v_dim] / acca[v_dim:v_dim + 1]
        ob = accb[:v_dim] / accb[v_dim:v_dim + 1]
        if mode == "diff":
            lam = lam_ref[...]
            lam_full = (jnp.exp(jnp.sum(lam[0:1] * lam[1:2], axis=-1, keepdims=True))
                        - jnp.exp(jnp.sum(lam[2:3] * lam[3:4], axis=-1, keepdims=True)) + lam_init)
            o = oa - lam_full * ob
            o = o * lax.rsqrt(jnp.mean(o * o, axis=0, keepdims=True) + 1e-6) * g_ref[...] * (1.0 - lam_init)
        else:
            o = jnp.concatenate([oa, ob], axis=0)
        o_ref[0] = _bf(o.T)

    n_blocks = n_keys // ATTN_KB
    if first_is_ctx:
        i = pl.program_id(2)

        @pl.when(i == 0)
        def _():
            finish(sweep(1, CTX_LEN // ATTN_KB))

        @pl.when(i > 0)
        def _():
            finish(sweep(n_blocks // ATTN_NB, ATTN_NB))
    else:
        finish(sweep(n_blocks // ATTN_NB, ATTN_NB))


def _attention(q, k, vt, extra, *, mode, want_ctx, lam_init=0.0):
    bsz, n, _ = k.shape
    tq = ROW_TILE
    off = 0 if want_ctx else 1
    n_out = n - off * tq
    qk_w = LANES if mode == "diff" else 2 * LANES
    in_specs = [pl.BlockSpec((1, tq, qk_w), lambda b, h, i: (b, i + off, h)),
                pl.BlockSpec((1, n, qk_w), lambda b, h, i: (b, 0, h)),
                pl.BlockSpec((1, n // ATTN_KB, LANES, ATTN_KB), lambda b, h, i: (b, 0, h, 0))]
    in_specs += [pl.BlockSpec(e.shape, lambda b, h, i: (0, 0)) for e in extra]
    return pl.pallas_call(
        functools.partial(_attn_kernel, mode=mode, n_keys=n, first_is_ctx=want_ctx, lam_init=lam_init),
        grid=(bsz, GROUP_W // LANES, n_out // tq),
        in_specs=in_specs,
        out_specs=pl.BlockSpec((1, tq, LANES), lambda b, h, i: (b, i, h)),
        out_shape=jax.ShapeDtypeStruct((bsz, n_out, GROUP_W), BF16),
        scratch_shapes=[pltpu.VMEM((2, 2, ATTN_NB * ATTN_KB, tq), F32)],
        compiler_params=pltpu.CompilerParams(dimension_semantics=("parallel", "parallel", "arbitrary"),
                                             vmem_limit_bytes=VMEM_LIMIT),
        name="attn_" + mode,
    )(q, k, vt, *extra)


def _halo_valid(i, n_tiles):
    return i >= 2, (i >= 1) & (i < n_tiles - 1)


def _rwkv_prep_kernel(p_ref, prev_ref, next_ref, sw_ref, w0_ref, wup_ref, a0_ref, aup_ref, gup_ref, kkw_ref, ka_ref,
                      ssum_ref, r_ref, k_ref, v_ref, kk_ref, lw_ref, a_ref, kd_ref, g_ref):
    i = pl.program_id(1)
    has_prev, has_next = _halo_valid(i, pl.num_programs(1))
    x = p_ref[0]
    prev = jnp.where(has_prev, prev_ref[0], 0.0)
    nxt = jnp.where(has_next, next_ref[0], 0.0)
    xm1, x0, xp1 = _shifted_rows(x, prev, nxt, (1, 0, -1))
    p = sw_ref[0:1, :] * xm1 + sw_ref[1:2, :] * x0 + sw_ref[2:3, :] * xp1

    r, k, v = p[:, 0:512], p[:, 512:1024], p[:, 1024:1536]
    w_raw = w0_ref[...] + _dot(_bf(jnp.tanh(p[:, 1536:1664])), wup_ref[...])
    lw = -RWKV_DECAY_SCALE * jax.nn.sigmoid(w_raw)
    a = jax.nn.sigmoid(a0_ref[...] + _dot(_bf(p[:, 1664:1792]), aup_ref[...]))
    kk = k * kkw_ref[...]
    kk = kk / jnp.maximum(jnp.sqrt(_group_stat(kk * kk, ssum_ref[...])), 1e-12)
    k2 = jnp.concatenate([k, k], axis=1)
    ka2 = jnp.concatenate([ka_ref[...]] * 2, axis=1)
    r_ref[0] = r
    k_ref[0] = k
    v_ref[0] = v
    kk_ref[0] = kk
    lw_ref[0] = lw
    a_ref[0] = a
    kd_ref[0] = k2 * (1.0 + (a - 1.0) * ka2)
    g_ref[0] = _dot(_bf(jax.nn.sigmoid(p[:, 1792:1920])), gup_ref[...])


def _halo_specs(width, tm):
    per = tm // HALO
    return [pl.BlockSpec((1, tm, width), lambda b, i: (b, i, 0)),
            pl.BlockSpec((1, HALO, width), lambda b, i: (b, jnp.maximum(i * per - 1, 0), 0)),
            pl.BlockSpec((1, HALO, width), lambda b, i: (b, jnp.minimum((i + 1) * per, pl.num_programs(1) * per - 1), 0))]


def _rwkv_prep(p, consts):
    bsz, n, _ = p.shape
    tm = ROW_TILE
    n_t = n // tm
    per = tm // HALO
    row = lambda b, i: (b, i, 0)
    const = lambda b, i: (0, 0)
    in_specs = [pl.BlockSpec((1, tm, P_RWKV), row),
                pl.BlockSpec((1, HALO, P_RWKV), lambda b, i: (b, jnp.maximum(i * per - 1, 0), 0)),
                pl.BlockSpec((1, HALO, P_RWKV), lambda b, i: (b, jnp.minimum((i + 1) * per, n_t * per - 1), 0))]
    in_specs += [pl.BlockSpec(c.shape, const) for c in consts]
    widths = (512, 512, 512, 512, 1024, 1024, 1024, 512)
    return pl.pallas_call(
        _rwkv_prep_kernel,
        grid=(bsz, n_t),
        in_specs=in_specs,
        out_specs=[pl.BlockSpec((1, tm, w), row) for w in widths],
        out_shape=[jax.ShapeDtypeStruct((bsz, n, w), F32) for w in widths],
        compiler_params=pltpu.CompilerParams(dimension_semantics=("parallel", "parallel"),
                                             vmem_limit_bytes=VMEM_LIMIT),
        name="rwkv_prep",
    )(p, p, p, *consts)


def _rwkv_scan_kernel(*refs, bsz):
    c = RWKV_CHUNK
    ins, y_refs, gt_sc = refs[:12], refs[12:14], refs[14]

    @pl.when(pl.program_id(0) == 0)
    def _():
        gt_sc[...] = jnp.zeros_like(gt_sc)

    ti = lax.broadcasted_iota(jnp.int32, (c, c), 0)
    si = lax.broadcasted_iota(jnp.int32, (c, c), 1)
    t2 = lax.broadcasted_iota(jnp.int32, (2 * c, 2 * c), 0)
    s2 = lax.broadcasted_iota(jnp.int32, (2 * c, 2 * c), 1)
    same_head = (t2 & c) == (s2 & c)
    eye = jnp.where(t2 == s2, 1.0, 0.0)
    m0 = lax.broadcasted_iota(jnp.int32, (1, LANES), 1) < HEAD_DIM

    def masks(fwd):
        tri = jnp.where((ti - si if fwd else si - ti) >= 0, 1.0, 0.0).astype(BF16)
        lag = jnp.where(same_head, t2 - s2 if fwd else s2 - t2, -1)
        return tri, lag > 0, lag >= 0

    def stack(x):
        return jnp.concatenate([jnp.where(m0, x, 0.0), jnp.where(m0, 0.0, x)], axis=0)

    chains = []
    for d in range(2):
        r_ref, kk_ref, v_ref, lw_ref, a_ref, kd_ref = ins[6 * d:6 * d + 6]
        tri, strict, incl = masks(d == 0)
        for b in range(bsz):
            for p in range(GROUP_W // LANES):
                sl = slice(p * LANES, (p + 1) * LANES)
                lw = lw_ref[b, :, sl]
                cum = _dot_exact_lhs(tri, lw)
                e_in, e_ex, e_neg = jnp.exp(cum), jnp.exp(cum - lw), jnp.exp(-cum)
                pe = jnp.exp(cum[c - 1:c, :] if d == 0 else cum[0:1, :])
                kk = kk_ref[b, :, sl]
                rb = stack(-(kk * a_ref[b, :, sl]) * e_neg)
                rk = stack(kd_ref[b, :, sl] * e_neg)
                vs = stack(v_ref[b, :, sl])
                chains.append(dict(
                    strict=strict, incl=incl, p_end=pe, vs=vs, vs_b=_bf(vs),
                    la=_bf(stack(kk * e_ex)), lr=_bf(stack(r_ref[b, :, sl] * e_in)), rb=_bf(rb), rk=_bf(rk),
                    rr=_bf(jnp.concatenate([rb * pe, rk * pe], axis=0)),
                    state=(d * bsz + b) * (GROUP_W // LANES) + p, out=(y_refs[d], b, sl)))

    for ch in chains:
        ch["a_ab"] = jnp.where(ch["strict"], _dot_nt(ch["la"], ch["rb"]), 0.0)
        ch["a_ak"] = _bf(jnp.where(ch["strict"], _dot_nt(ch["la"], ch["rk"]), 0.0))
        ch["a_rb"] = _bf(jnp.where(ch["incl"], _dot_nt(ch["lr"], ch["rb"]), 0.0))
        ch["a_rk"] = _bf(jnp.where(ch["incl"], _dot_nt(ch["lr"], ch["rk"]), 0.0))

    for ch in chains:
        ch["tm"] = eye + ch["a_ab"]
        ch["apow"] = ch["a_ab"]
    for _ in range(5):
        for ch in chains:
            ap = _bf(ch["apow"])
            ch["apow"] = _dot(ap, ap)
        for ch in chains:
            ch["tm"] = ch["tm"] + _dot(_bf(ch["tm"]), _bf(ch["apow"]))
    for ch in chains:
        ch["tm"] = _bf(ch["tm"])
        ch["akv"] = _bf(_dot(ch["a_ak"], ch["vs_b"]))
    for ch in chains:
        ch["w"] = _bf(_dot(ch["tm"], ch["la"]))
        ch["u0"] = _dot(ch["tm"], ch["akv"])
        ch["y0"] = _dot(ch["a_rk"], ch["vs_b"])

    for ch in chains:
        ch["gt"] = gt_sc[ch["state"]]
        ch["gt_b"] = _bf(ch["gt"])
        ch["u"] = _dot_nt(ch["w"], ch["gt_b"]) + ch["u0"]
    for ch in chains:
        y = _dot_nt(ch["lr"], ch["gt_b"]) + _dot(ch["a_rb"], _bf(ch["u"])) + ch["y0"]
        y_ref, b, sl = ch["out"]
        y_ref[b, :, sl] = y[:c] + y[c:]
        uv = jnp.concatenate([ch["u"], ch["vs"]], axis=0)
        gt_sc[ch["state"]] = ch["gt"] * ch["p_end"] + _dot(_bf(uv.T), ch["rr"])


def _bwd_chunk_index(s, n_ctx, n_all):
    return jnp.where(s < n_ctx, n_ctx - 1 - s, n_all + n_ctx - 1 - s)


def _rwkv_scan(r, kk, v, lw, a, kd):
    bsz, n, _ = r.shape
    c = RWKV_CHUNK
    n_all, n_ctx = n // c, CTX_LEN // c
    blk = (bsz, c, GROUP_W)
    fwd = [pl.BlockSpec(blk, lambda s: (0, s, 0))] * 6
    bwd = ([pl.BlockSpec(blk, lambda s: (0, _bwd_chunk_index(s, n_ctx, n_all), 0))] * 3
           + [pl.BlockSpec(blk, lambda s: (0, _bwd_chunk_index(s, n_ctx, n_all), 1))] * 3)
    return pl.pallas_call(
        functools.partial(_rwkv_scan_kernel, bsz=bsz),
        grid=(n_all,),
        in_specs=fwd + bwd,
        out_specs=[pl.BlockSpec(blk, lambda s: (0, s, 0)),
                   pl.BlockSpec(blk, lambda s: (0, _bwd_chunk_index(s, n_ctx, n_all), 0))],
        out_shape=[jax.ShapeDtypeStruct((bsz, n, GROUP_W), F32)] * 2,
        scratch_shapes=[pltpu.VMEM((2 * bsz * (GROUP_W // LANES), LANES, LANES), F32)],
        compiler_params=pltpu.CompilerParams(dimension_semantics=("arbitrary",), vmem_limit_bytes=VMEM_LIMIT),
        name="rwkv_scan",
    )(r, kk, v, lw, a, kd, r, kk, v, lw, a, kd)


def _ssd_prep_kernel(p_ref, prev_ref, next_ref, cw_ref, cb_ref, dtb_ref, aneg_ref, e_ref, pa_ref,
                     xs_ref, bc_ref, xdt_ref, adt_ref):
    i = pl.program_id(1)
    has_prev, has_next = _halo_valid(i, pl.num_programs(1))
    x = p_ref[0, :, 512:1536]
    prev = jnp.where(has_prev, prev_ref[0, :, 512:1536], 0.0)
    nxt = jnp.where(has_next, next_ref[0, :, 512:1536], 0.0)
    taps = _shifted_rows(x, prev, nxt, (2, 1, 0, -1, -2))
    conv = cb_ref[...]
    for j, tap in enumerate(taps):
        conv = conv + cw_ref[j:j + 1, :] * tap
    xbc = _silu(conv)
    xs = xbc[:, 0:512]
    xs_ref[0] = xs
    bc_ref[0] = xbc[:, 512:1024]
    t = p_ref[0, :, 1536:1664] + dtb_ref[...]
    dt = jnp.maximum(t, 0.0) + jnp.log(1.0 + jnp.exp(-jnp.abs(t)))
    xdt_ref[0] = jnp.concatenate([xs, xs], axis=1) * _dot_exact_rhs(dt, e_ref[...])
    adt_ref[0] = _dot_exact_rhs(dt * aneg_ref[...], pa_ref[...])


def _ssd_prep(p, consts):
    bsz, n, _ = p.shape
    tm = ROW_TILE
    n_t = n // tm
    per = tm // HALO
    row = lambda b, i: (b, i, 0)
    const = lambda b, i: (0, 0)
    in_specs = [pl.BlockSpec((1, tm, P_SSD), row),
                pl.BlockSpec((1, HALO, P_SSD), lambda b, i: (b, jnp.maximum(i * per - 1, 0), 0)),
                pl.BlockSpec((1, HALO, P_SSD), lambda b, i: (b, jnp.minimum((i + 1) * per, n_t * per - 1), 0))]
    in_specs += [pl.BlockSpec(c.shape, const) for c in consts]
    widths = (512, 512, 1024, 256)
    return pl.pallas_call(
        _ssd_prep_kernel,
        grid=(bsz, n_t),
        in_specs=in_specs,
        out_specs=[pl.BlockSpec((1, tm, w), row) for w in widths],
        out_shape=[jax.ShapeDtypeStruct((bsz, n, w), F32) for w in widths],
        compiler_params=pltpu.CompilerParams(dimension_semantics=("parallel", "parallel"),
                                             vmem_limit_bytes=VMEM_LIMIT),
        name="ssd_prep",
    )(p, p, p, *consts)


def _ssd_scan_kernel(*refs, bsz):
    q = SSD_CHUNK
    ins, y_refs, ht_sc = refs[:6], refs[6:8], refs[8]

    @pl.when(pl.program_id(0) == 0)
    def _():
        ht_sc[...] = jnp.zeros_like(ht_sc)

    ti = lax.broadcasted_iota(jnp.int32, (q, q), 0)
    si = lax.broadcasted_iota(jnp.int32, (q, q), 1)
    m0 = lax.broadcasted_iota(jnp.int32, (1, LANES), 1) < HEAD_DIM

    def per_head(fn, h0):
        return jnp.where(m0, fn(h0), fn(h0 + 1))

    for d in range(2):
        x_ref, a_ref, bc_ref = ins[3 * d:3 * d + 3]
        tri = (ti - si if d == 0 else si - ti) >= 0
        tri_b = jnp.where(tri, 1.0, 0.0).astype(BF16)
        for b in range(bsz):
            acum = _dot_exact_lhs(tri_b, a_ref[b])
            acum_t = acum.T
            last = acum[q - 1:q, :] if d == 0 else acum[0:1, :]
            bc = bc_ref[b]

            def decay(h, acum=acum, acum_t=acum_t, tri=tri):
                seg = acum[:, h:h + 1] - acum_t[h:h + 1, :]
                return jnp.where(tri, jnp.exp(jnp.where(tri, seg, 0.0)), 0.0)

            for g in range(2):
                bg = bc[:, g * SSD_STATE:(g + 1) * SSD_STATE]
                cg_b = _bf(bc[:, (2 + g) * SSD_STATE:(3 + g) * SSD_STATE])
                scores = _dot_nt(cg_b, _bf(bg))
                bg_t = _bf(bg.T)
                for pp in range(2):
                    pair = 2 * g + pp
                    h0 = 2 * pair
                    state = (d * bsz + b) * (GROUP_W // LANES) + pair
                    sl = slice(pair * LANES, (pair + 1) * LANES)
                    xp = x_ref[b, :, sl]
                    xp_b = _bf(xp)
                    y0 = _dot(_bf(scores * decay(h0)), xp_b)
                    y1 = _dot(_bf(scores * decay(h0 + 1)), xp_b)
                    ht = ht_sc[state]
                    carry = _dot(cg_b, _bf(ht)) * per_head(lambda h: jnp.exp(acum[:, h:h + 1]), h0)
                    y_refs[d][b, :, sl] = jnp.where(m0, y0, y1) + carry
                    to_end = per_head(lambda h: jnp.exp(last[:, h:h + 1] - acum[:, h:h + 1]), h0)
                    e_last = per_head(lambda h: jnp.exp(last[:, h:h + 1]), h0)
                    ht_sc[state] = ht * e_last + _dot(bg_t, _bf(xp * to_end))


def _ssd_scan(xdt, adt, bc):
    bsz, n, _ = bc.shape
    q = SSD_CHUNK
    n_all, n_ctx = n // q, CTX_LEN // q
    bwd = lambda s: _bwd_chunk_index(s, n_ctx, n_all)
    wide, narrow = (bsz, q, GROUP_W), (bsz, q, LANES)
    in_specs = [pl.BlockSpec(wide, lambda s: (0, s, 0)), pl.BlockSpec(narrow, lambda s: (0, s, 0)),
                pl.BlockSpec(wide, lambda s: (0, s, 0)),
                pl.BlockSpec(wide, lambda s: (0, bwd(s), 1)), pl.BlockSpec(narrow, lambda s: (0, bwd(s), 1)),
                pl.BlockSpec(wide, lambda s: (0, bwd(s), 0))]
    return pl.pallas_call(
        functools.partial(_ssd_scan_kernel, bsz=bsz),
        grid=(n_all,),
        in_specs=in_specs,
        out_specs=[pl.BlockSpec(wide, lambda s: (0, s, 0)), pl.BlockSpec(wide, lambda s: (0, bwd(s), 0))],
        out_shape=[jax.ShapeDtypeStruct((bsz, n, GROUP_W), F32)] * 2,
        scratch_shapes=[pltpu.VMEM((2 * bsz * (GROUP_W // LANES), SSD_STATE, LANES), F32)],
        compiler_params=pltpu.CompilerParams(dimension_semantics=("arbitrary",), vmem_limit_bytes=VMEM_LIMIT),
        name="ssd_scan",
    )(xdt, adt, bc, xdt, adt, bc)


def _post_kernel(z_ref, mctx_ref, mlat_ref, od_ref, om_ref, yrf_ref, yrb_ref, r_ref, k_ref, v_ref, g_ref,
                 lng_ref, lnb_ref, rk_ref, ysf_ref, ysb_ref, xs_ref, zg_ref, dsk_ref, ng_ref, smean_ref, ssum_ref,
                 w_ref, o_ref, *, tile_off):
    is_ctx = _is_ctx_rows(pl.program_id(1) + tile_off, ROW_TILE, CTX_LEN)
    smean = smean_ref[...]
    y = yrf_ref[0] + yrb_ref[0]
    yc = y - _group_stat(y, smean)
    yn = yc * lax.rsqrt(_group_stat(yc * yc, smean) + RWKV_LN_EPS) * lng_ref[...] + lnb_ref[...]
    bonus = _group_stat(r_ref[0] * k_ref[0] * rk_ref[...], ssum_ref[...]) * v_ref[0]
    o_rwkv = (yn + bonus) * g_ref[0]

    t = (ysf_ref[0] + ysb_ref[0] + dsk_ref[...] * xs_ref[0]) * _silu(zg_ref[0])
    o_ssd = t * lax.rsqrt(jnp.mean(t * t, axis=-1, keepdims=True) + 1e-6) * ng_ref[...]

    mix = (_dot(od_ref[0], w_ref[0:512, :]) + _dot(_bf(o_rwkv), w_ref[512:1024, :])
           + _dot(om_ref[0], w_ref[1024:1536, :]) + _dot(_bf(o_ssd), w_ref[1536:2048, :]))
    o_ref[0] = z_ref[0] + _mod_row(mctx_ref, mlat_ref, 5, is_ctx) * mix


def _post(z, m, od, om, yr, r, k, v, g, ys, xs, p_ssd, consts, w, *, want_ctx):
    bsz, n, d = z.shape
    tm = ROW_TILE
    off = 0 if want_ctx else 1
    n_out = n - off * tm
    full = lambda b, i: (b, i + off, 0)
    attn = lambda b, i: (b, i, 0)
    const = lambda b, i: (0, 0)
    seq = lambda width: pl.BlockSpec((1, tm, width), full)
    in_specs = [seq(d),
                pl.BlockSpec((1, N_MOD, d), lambda b, i: (0, 0, 0)),
                pl.BlockSpec((1, N_MOD, d), lambda b, i: (b + 1, 0, 0)),
                pl.BlockSpec((1, tm, GROUP_W), attn), pl.BlockSpec((1, tm, GROUP_W), attn),
                seq(GROUP_W), seq(GROUP_W), seq(GROUP_W), seq(GROUP_W), seq(GROUP_W), seq(GROUP_W)]
    in_specs += [pl.BlockSpec(c.shape, const) for c in consts[:3]]
    in_specs += [seq(GROUP_W), seq(GROUP_W), seq(GROUP_W), seq(GROUP_W)]
    in_specs += [pl.BlockSpec(c.shape, const) for c in consts[3:]]
    in_specs += [pl.BlockSpec(w.shape, const, pipeline_mode=pl.Buffered(1))]
    return pl.pallas_call(
        functools.partial(_post_kernel, tile_off=off),
        grid=(bsz, n_out // tm),
        in_specs=in_specs,
        out_specs=pl.BlockSpec((1, tm, d), lambda b, i: (b, i, 0)),
        out_shape=jax.ShapeDtypeStruct((bsz, n_out, d), F32),
        compiler_params=pltpu.CompilerParams(dimension_semantics=("parallel", "parallel"),
                                             vmem_limit_bytes=VMEM_LIMIT),
        name="post",
    )(z, m, m, od, om, yr[0], yr[1], r, k, v, g, *consts[:3], ys[0], ys[1], xs, p_ssd, *consts[3:], w)


def _pad_cols(w, width):
    return jnp.pad(w, ((0, 0), (0, width - w.shape[1])))


def _block_diag_const(block, reps):
    return jnp.asarray(np.kron(np.eye(reps, dtype=np.float32), block), BF16)


def _group_consts():
    ones64 = np.ones((64, 64), np.float32)
    mla_blk = np.zeros((LANES, LANES), np.float32)
    mla_blk[:MLA_NOPE, :MLA_NOPE] = 1.0 / MLA_NOPE
    mla_blk[MLA_NOPE:MLA_NOPE + MLA_ROPE, MLA_NOPE:MLA_NOPE + MLA_ROPE] = 1.0 / MLA_ROPE
    return (_block_diag_const(ones64 / 64.0, 4), _block_diag_const(ones64, 4), _block_diag_const(mla_blk, 2))


def _rope_angles(n_lat, dim):
    quarter = dim // 4
    inv = ROPE_BASE ** (-jnp.arange(quarter, dtype=F32) / quarter)
    rows = n_lat // GRID_W
    pos_r = jnp.repeat(jnp.arange(rows), GRID_W).astype(F32)
    pos_c = jnp.tile(jnp.arange(GRID_W), rows).astype(F32)
    ang = jnp.concatenate([pos_r[:, None] * inv, pos_c[:, None] * inv], axis=-1)
    return jnp.cos(ang), jnp.sin(ang)


def _rope_lane_tables(n_lat, dim, lane_off):
    cos, sin = _rope_angles(n_lat, dim)
    cos_t = jnp.ones((n_lat, LANES), F32).at[:, lane_off:lane_off + dim].set(jnp.concatenate([cos, cos], axis=-1))
    sin_t = jnp.zeros((n_lat, LANES), F32).at[:, lane_off:lane_off + dim].set(jnp.concatenate([-sin, sin], axis=-1))
    cos_t = jnp.concatenate([jnp.ones((CTX_LEN, LANES), F32), cos_t], axis=0)
    sin_t = jnp.concatenate([jnp.zeros((CTX_LEN, LANES), F32), sin_t], axis=0)
    return cos_t, sin_t


def _row(x):
    return x.reshape(1, -1).astype(F32)


def _mla_consts(q_norm_g, kv_norm_g, w_uq, w_ukv, nope_g, rope_g, s_mla):
    hq = MLA_NOPE + MLA_ROPE
    w_uq = w_uq.reshape(MLA_Q_RANK, MLA_HEADS, hq)
    w_uq = jnp.pad(w_uq, ((0, 0), (0, 0), (0, LANES - hq))).reshape(MLA_Q_RANK, MLA_HEADS * LANES)
    w_ukv = w_ukv.reshape(MLA_KV_RANK, MLA_HEADS, 2 * MLA_NOPE)
    w_k = jnp.pad(w_ukv[:, :, :MLA_NOPE], ((0, 0), (0, 0), (0, LANES - MLA_NOPE))).reshape(MLA_KV_RANK, -1)
    w_v = w_ukv[:, :, MLA_NOPE:].reshape(MLA_KV_RANK, GROUP_W)
    zpad = jnp.zeros((LANES - hq,), F32)
    g_q = jnp.tile(jnp.concatenate([nope_g[0], rope_g[0], zpad]), MLA_HEADS)
    g_k = jnp.tile(jnp.concatenate([nope_g[1], jnp.zeros((LANES - MLA_NOPE,), F32)]), MLA_HEADS)
    g_kr = jnp.concatenate([rope_g[1], jnp.zeros((LANES - MLA_ROPE,), F32)])
    place = np.zeros((LANES, MLA_HEADS * LANES), np.float32)
    for h in range(MLA_HEADS):
        place[np.arange(MLA_ROPE), h * LANES + MLA_NOPE + np.arange(MLA_ROPE)] = 1.0
    return (_row(q_norm_g), _row(kv_norm_g), _bf(w_uq), _bf(w_k), _bf(w_v), _row(g_q), _row(g_k), _row(g_kr),
            jnp.asarray(place, BF16), s_mla)


def _two_dir_block_diag(w):
    z = jnp.zeros_like(w[0])
    return jnp.concatenate([jnp.concatenate([w[0], z], axis=1), jnp.concatenate([z, w[1]], axis=1)], axis=0)


def _ssd_expand_consts():
    e = np.zeros((LANES, 2 * GROUP_W), np.float32)
    pa = np.zeros((LANES, 2 * LANES), np.float32)
    for d in range(2):
        for h in range(SSD_HEADS):
            e[d * SSD_HEADS + h, d * GROUP_W + h * HEAD_DIM:d * GROUP_W + (h + 1) * HEAD_DIM] = 1.0
            pa[d * SSD_HEADS + h, d * LANES + h] = 1.0
    return jnp.asarray(e, BF16), jnp.asarray(pa, BF16)


def _pad_lanes(v):
    return jnp.pad(v.reshape(1, -1).astype(F32), ((0, 0), (0, LANES - v.size)))


def kernel(x, c, ctx, c_ctx, mod_w, mod_b, norm_g, ffn_w_gate, ffn_w_up, ffn_w_down, w_in, w_out, diff_qk_g, diff_lambda, diff_subln_g, rwkv_shift_w, rwkv_w0, rwkv_w_up, rwkv_a0, rwkv_a_up, rwkv_g_up, rwkv_k_k, rwkv_k_a, rwkv_r_k, rwkv_ln_g, rwkv_ln_b, mla_q_norm_g, mla_kv_norm_g, mla_w_uq, mla_w_ukv, mla_nope_g, mla_rope_g, ssd_conv_w, ssd_conv_b, ssd_dt_bias, ssd_a_log, ssd_d, ssd_norm_g):
    bsz, n_lat, d = x.shape
    depth = mod_w.shape[0]
    s_mean64, s_sum64, s_mla = _group_consts()
    e_dt, p_adt = _ssd_expand_consts()
    cos_d, sin_d = _rope_lane_tables(n_lat, HEAD_DIM, 0)
    cos_d = cos_d.at[:, HEAD_DIM:].set(cos_d[:, :HEAD_DIM])
    sin_d = sin_d.at[:, HEAD_DIM:].set(sin_d[:, :HEAD_DIM])
    cos_ma, sin_ma = _rope_lane_tables(n_lat, MLA_ROPE, MLA_NOPE)
    cos_mb, sin_mb = _rope_lane_tables(n_lat, MLA_ROPE, 0)

    cvec = jnp.zeros((8, d), F32).at[0].set(c_ctx).at[1:1 + bsz].set(c)
    z = jnp.concatenate([ctx, x], axis=1)
    tm_ffn = 640
    wg_all, wu_all, wd_all = _bf(ffn_w_gate), _bf(ffn_w_up), _bf(ffn_w_down)

    for l in range(depth):
        want_ctx = l < depth - 1
        lam_init = 0.8 - 0.6 * float(np.exp(-0.3 * l))
        m = _mod(cvec, mod_w, mod_b.reshape(depth, 1, N_MOD * d), layer=l).reshape(8, N_MOD, d)

        z = _ffn(z, m, _row(norm_g[l, 0]), wg_all, wu_all, wd_all, layer=l, which=0,
                 j=0, ctx_len=CTX_LEN, tm=tm_ffn)

        w = w_in[l]
        w_p = jnp.concatenate([w[:, :3456], _pad_cols(w[:, 3456:4000], P_MLA), _pad_cols(w[:, 4000:], P_SSD)], axis=1)
        qd, kd, vd, p_rwkv, qm, km, vm, p_ssd = _inproj(
            z, m, _row(norm_g[l, 1]), _bf(w_p),
            (jnp.tile(diff_qk_g[l], (1, 2 * DIFF_HEADS)), s_mean64), (cos_d, sin_d),
            _mla_consts(mla_q_norm_g[l], mla_kv_norm_g[l], mla_w_uq[l], mla_w_ukv[l], mla_nope_g[l], mla_rope_g[l],
                        s_mla),
            (cos_ma, sin_ma, cos_mb, sin_mb))
        o_diff = _attention(qd, kd, vd, (diff_lambda[l], diff_subln_g[l].reshape(-1, 1)), mode="diff", want_ctx=want_ctx,
                            lam_init=lam_init)
        o_mla = _attention(qm, km, vm, (), mode="mla", want_ctx=want_ctx)

        rw_consts = (rwkv_shift_w[l], _row(rwkv_w0[l]), _bf(_two_dir_block_diag(rwkv_w_up[l])), _row(rwkv_a0[l]),
                     _bf(_two_dir_block_diag(rwkv_a_up[l])), _bf(rwkv_g_up[l]), _row(rwkv_k_k[l]),
                     _row(rwkv_k_a[l]), s_sum64)
        r, k, v, kk, lw, a, kdir, g = _rwkv_prep(p_rwkv, rw_consts)
        y_rwkv = _rwkv_scan(r, kk, v, lw, a, kdir)

        sd_consts = (ssd_conv_w[l], _row(ssd_conv_b[l]), _pad_lanes(ssd_dt_bias[l]),
                     _pad_lanes(-jnp.exp(ssd_a_log[l])), e_dt, p_adt)
        xs, bc, xdt, adt = _ssd_prep(p_ssd, sd_consts)
        y_ssd = _ssd_scan(xdt, adt, bc)

        post_consts = (_row(rwkv_ln_g[l]), _row(rwkv_ln_b[l]), _row(rwkv_r_k[l]),
                       _row(jnp.repeat(ssd_d[l], HEAD_DIM)), _row(ssd_norm_g[l]), s_mean64, s_sum64)
        z = _post(z, m, o_diff, o_mla, y_rwkv, r, k, v, g, y_ssd, xs, p_ssd, post_consts, _bf(w_out[l]),
                  want_ctx=want_ctx)

        z = _ffn(z, m, _row(norm_g[l, 2]), wg_all, wu_all, wd_all, layer=l, which=1,
                 j=2, ctx_len=CTX_LEN if want_ctx else 0, tm=tm_ffn if want_ctx else 512)
    return z
```

```python
import functools

import jax
import jax.numpy as jnp
import numpy as np
from jax import lax
from jax.experimental import pallas as pl
from jax.experimental.pallas import tpu as pltpu

F32 = jnp.float32
BF16 = jnp.bfloat16

CTX_LEN = 256
GRID_W = 64
HEAD_DIM = 64
ROPE_BASE = 10000.0
N_MOD = 9
GROUP_W = 512
DIFF_HEADS = 4
MLA_HEADS = 8
MLA_NOPE = 64
MLA_ROPE = 32
MLA_Q_RANK = 384
MLA_KV_RANK = 128
RWKV_DECAY_SCALE = 0.606531
RWKV_LN_EPS = 64e-5
RWKV_CHUNK = 64
SSD_HEADS = 8
SSD_STATE = 128
SSD_CHUNK = 128
LOG2E = 1.4426950408889634

LANES = 128
ROW_TILE = CTX_LEN
ADALN_ROWS = 16
FFN_AHEAD_ROWS = 64
HALO = 8
ATTN_KB = ROW_TILE
ATTN_NB = 5
ONES_ROWS = 16
VMEM_LIMIT = 56 * 1024 * 1024

P_DIFF = 1536
P_RWKV = 1920
P_MLA = 640
P_SSD = 1664
P_OFFS = (0, P_DIFF, P_DIFF + P_RWKV, P_DIFF + P_RWKV + P_MLA, P_DIFF + P_RWKV + P_MLA + P_SSD)


def _dot(a, b):
    return jnp.dot(a, b, preferred_element_type=F32)


def _dot_nt(a, b):
    return lax.dot_general(a, b, (((1,), (1,)), ((), ())), preferred_element_type=F32)


def _bf(x):
    return x.astype(BF16)


def _split3(x):
    x1 = x.astype(BF16)
    r1 = x - x1.astype(F32)
    x2 = r1.astype(BF16)
    r2 = r1 - x2.astype(F32)
    return x1, x2, r2.astype(BF16)


def _dot_exact_lhs(a, x):
    x1, x2, x3 = _split3(x)
    return _dot(a, x1) + _dot(a, x2) + _dot(a, x3)


def _dot_exact_rhs(x, a):
    x1, x2, x3 = _split3(x)
    return _dot(x1, a) + _dot(x2, a) + _dot(x3, a)


def _group_stat(x, s):
    hi = x.astype(BF16)
    lo = (x - hi.astype(F32)).astype(BF16)
    blk = s.shape[0]
    outs = []
    for c in range(x.shape[-1] // blk):
        sl = slice(c * blk, (c + 1) * blk)
        outs.append(_dot(hi[:, sl], s) + _dot(lo[:, sl], s))
    return outs[0] if len(outs) == 1 else jnp.concatenate(outs, axis=-1)


def _swap_halves(x, half):
    n = x.shape[-1]
    lane = lax.broadcasted_iota(jnp.int32, (1, n), 1)
    up = pltpu.roll(x, n - half, axis=1)
    dn = pltpu.roll(x, half, axis=1)
    return jnp.where((lane & (2 * half - 1)) < half, up, dn)


def _silu(x):
    return x * jax.nn.sigmoid(x)


def _is_ctx_rows(tile_idx, tm, ctx_len):
    rows = tile_idx * tm + lax.broadcasted_iota(jnp.int32, (tm, 1), 0)
    return rows < ctx_len


def _mod_row(mctx_ref, mlat_ref, idx, is_ctx):
    return jnp.where(is_ctx, mctx_ref[0, idx:idx + 1, :], mlat_ref[0, idx:idx + 1, :])


def _adaln(z, g, mctx_ref, mlat_ref, j, is_ctx):
    ms = jnp.mean(z * z, axis=-1, keepdims=True)
    h = z * lax.rsqrt(ms + 1e-6) * g
    return h * (1.0 + _mod_row(mctx_ref, mlat_ref, 3 * j + 1, is_ctx)) + _mod_row(mctx_ref, mlat_ref, 3 * j, is_ctx)


def _adaln_to_scratch(z_ref, g_ref, mctx_ref, mlat_ref, j, tile_idx, tm, ctx_len, h_sc):
    assert ctx_len % ADALN_ROWS == 0 and tm % ADALN_ROWS == 0
    g = g_ref[...]

    def rows(r, carry):
        sl = pl.ds(pl.multiple_of(r * ADALN_ROWS, ADALN_ROWS), ADALN_ROWS)
        step_is_ctx = tile_idx * tm + r * ADALN_ROWS < ctx_len
        h_sc[sl, :] = _bf(_adaln(z_ref[0, sl, :], g, mctx_ref, mlat_ref, j, step_is_ctx))
        return carry

    lax.fori_loop(0, tm // ADALN_ROWS, rows, 0, unroll=4)


def _shifted_rows(x, prev, nxt, shifts):
    tm = x.shape[0]
    ext = jnp.concatenate([prev, x, nxt], axis=0)
    n = tm + 2 * HALO
    out = []
    for s in shifts:
        out.append(x if s == 0 else pltpu.roll(ext, s % n, axis=0)[HALO:HALO + tm])
    return out


def _mod_kernel(c_ref, w_ref, b_ref, o_ref):
    o_ref[...] = _dot(_bf(_silu(c_ref[...])), _bf(w_ref[...])) + b_ref[...]


def _mod(cvec, w, b, *, layer):
    _, d, n = w.shape
    tn = 1024
    return pl.pallas_call(
        _mod_kernel,
        grid=(n // tn,),
        in_specs=[pl.BlockSpec((8, d), lambda i: (0, 0)),
                  pl.BlockSpec((None, d, tn), lambda i: (layer, 0, i)),
                  pl.BlockSpec((None, 1, tn), lambda i: (layer, 0, i))],
        out_specs=pl.BlockSpec((8, tn), lambda i: (0, i)),
        out_shape=jax.ShapeDtypeStruct((8, n), F32),
        compiler_params=pltpu.CompilerParams(dimension_semantics=("parallel",), vmem_limit_bytes=VMEM_LIMIT),
        name="mod",
    )(cvec, w, b)


def _ffn_kernel(z_ref, zn_ref, mctx_ref, mlat_ref, g_ref, wg_ref, wu_ref, wd_ref, o_ref, h_sc, acc_sc, *,
                j, ctx_len, tm):
    i, f = pl.program_id(1), pl.program_id(2)
    cur = i % 2

    @pl.when(f == 0)
    def _():
        acc_sc[...] = jnp.zeros_like(acc_sc)

    @pl.when((f == 0) & (i == 0))
    def _():
        _adaln_to_scratch(z_ref, g_ref, mctx_ref, mlat_ref, j, 0, tm, ctx_len, h_sc.at[0])

    h = h_sc[cur]
    act = _silu(_dot(h, wg_ref[...])) * _dot(h, wu_ref[...])
    acc_sc[...] += _dot(_bf(act), wd_ref[...])

    ahead = jnp.minimum(f, tm // FFN_AHEAD_ROWS - 1) * FFN_AHEAD_ROWS
    for r in range(0, FFN_AHEAD_ROWS, ADALN_ROWS):
        hn = _bf(_adaln(zn_ref[0, r:r + ADALN_ROWS, :], g_ref[...], mctx_ref, mlat_ref, j, False))
        h_sc[1 - cur, pl.ds(pl.multiple_of(ahead + r, ADALN_ROWS), ADALN_ROWS), :] = hn

    @pl.when(f == pl.num_programs(2) - 1)
    def _():
        gate = _mod_row(mctx_ref, mlat_ref, 3 * j + 2, _is_ctx_rows(pl.program_id(1), tm, ctx_len))
        o_ref[0] = z_ref[0] + gate * (0.5 * acc_sc[...])


def _ffn(z, m, g, wg, wu, wd, *, layer, which, j, ctx_len, tm):
    bsz, n, d = z.shape
    ff = wg.shape[-1]
    tf = 512
    per_tile = tm // FFN_AHEAD_ROWS
    assert per_tile <= ff // tf and n % tm == 0
    last_chunk = n // FFN_AHEAD_ROWS - 1

    def ahead_chunk(b, i, f):
        return b, jnp.minimum((i + 1) * per_tile + jnp.minimum(f, per_tile - 1), last_chunk), 0

    return pl.pallas_call(
        functools.partial(_ffn_kernel, j=j, ctx_len=ctx_len, tm=tm),
        grid=(bsz, n // tm, ff // tf),
        in_specs=[pl.BlockSpec((1, tm, d), lambda b, i, f: (b, i, 0)),
                  pl.BlockSpec((1, FFN_AHEAD_ROWS, d), ahead_chunk),
                  pl.BlockSpec((1, N_MOD, d), lambda b, i, f: (0, 0, 0)),
                  pl.BlockSpec((1, N_MOD, d), lambda b, i, f: (b + 1, 0, 0)),
                  pl.BlockSpec((1, d), lambda b, i, f: (0, 0)),
                  pl.BlockSpec((None, None, d, tf), lambda b, i, f: (layer, which, 0, f)),
                  pl.BlockSpec((None, None, d, tf), lambda b, i, f: (layer, which, 0, f)),
                  pl.BlockSpec((None, None, tf, d), lambda b, i, f: (layer, which, f, 0))],
        out_specs=pl.BlockSpec((1, tm, d), lambda b, i, f: (b, i, 0)),
        out_shape=jax.ShapeDtypeStruct(z.shape, F32),
        scratch_shapes=[pltpu.VMEM((2, tm, d), BF16), pltpu.VMEM((tm, d), F32)],
        compiler_params=pltpu.CompilerParams(dimension_semantics=("parallel", "arbitrary", "arbitrary"),
                                             vmem_limit_bytes=VMEM_LIMIT),
        name="ffn",
    )(z, z, m, m, g, wg, wu, wd)


N_DIFF_IN = 4
N_MLA_IN = 14


def _inproj_kernel(*refs):
    (z_ref, mctx_ref, mlat_ref, g_ref, w_ref), refs = refs[:5], refs[5:]
    diff_in, refs = refs[:N_DIFF_IN], refs[N_DIFF_IN:]
    mla_in, refs = refs[:N_MLA_IN], refs[N_MLA_IN:]
    qd_ref, kd_ref, vtd_ref, or_ref, qm_ref, km_ref, vtm_ref, os_ref = refs
    is_ctx = pl.program_id(1) * ROW_TILE < CTX_LEN
    h = _bf(_adaln(z_ref[0], g_ref[...], mctx_ref, mlat_ref, 1, is_ctx))

    def proj(group):
        return _dot(h, w_ref[:, P_OFFS[group]:P_OFFS[group + 1]])

    _diff_prep(proj(0), *diff_in, qd_ref, kd_ref, vtd_ref)
    or_ref[0] = proj(1)
    _mla_prep(proj(2), *mla_in, qm_ref, km_ref, vtm_ref)
    os_ref[0] = proj(3)


def _inproj(z, m, g, w, diff_consts, diff_tabs, mla_consts, mla_tabs):
    bsz, n, d = z.shape
    tm = ROW_TILE
    row = lambda b, i: (b, i, 0)
    const = lambda b, i: (0, 0)
    tab = pl.BlockSpec((tm, LANES), lambda b, i: (i, 0))
    mla_w = MLA_HEADS * LANES
    in_specs = [pl.BlockSpec((1, tm, d), row),
                pl.BlockSpec((1, N_MOD, d), lambda b, i: (0, 0, 0)),
                pl.BlockSpec((1, N_MOD, d), lambda b, i: (b + 1, 0, 0)),
                pl.BlockSpec((1, d), const),
                pl.BlockSpec(w.shape, const, pipeline_mode=pl.Buffered(1))]
    in_specs += [pl.BlockSpec(diff_consts[0].shape, const), tab, tab, pl.BlockSpec(diff_consts[1].shape, const)]
    in_specs += [pl.BlockSpec(c.shape, const) for c in mla_consts] + [tab] * 4
    seq = lambda width: pl.BlockSpec((1, tm, width), row)
    sds = lambda width, dt: jax.ShapeDtypeStruct((bsz, n, width), dt)
    return pl.pallas_call(
        _inproj_kernel,
        grid=(bsz, n // tm),
        in_specs=in_specs,
        out_specs=[seq(GROUP_W), seq(GROUP_W), _VT_SPEC, seq(P_RWKV), seq(mla_w), seq(mla_w), _VT_SPEC, seq(P_SSD)],
        out_shape=[sds(GROUP_W, BF16), sds(GROUP_W, BF16), _vt_shape(bsz, n), sds(P_RWKV, F32),
                   sds(mla_w, BF16), sds(mla_w, BF16), _vt_shape(bsz, n), sds(P_SSD, F32)],
        compiler_params=pltpu.CompilerParams(dimension_semantics=("parallel", "parallel"),
                                             vmem_limit_bytes=VMEM_LIMIT),
        name="inproj",
    )(z, m, m, g, w, diff_consts[0], *diff_tabs, diff_consts[1], *mla_consts, *mla_tabs)


_VT_SPEC = pl.BlockSpec((1, 1, GROUP_W, ROW_TILE), lambda b, i: (b, i, 0, 0))


def _vt_shape(bsz, n):
    return jax.ShapeDtypeStruct((bsz, n // ROW_TILE, GROUP_W, ROW_TILE), BF16)


def _diff_prep(p, g_ref, cos_ref, sin_ref, s_ref, q_ref, k_ref, vt_ref):
    s = s_ref[...]
    cos = jnp.concatenate([cos_ref[...]] * 4, axis=1)
    sin = jnp.concatenate([sin_ref[...]] * 4, axis=1)

    def norm_rope(x, g):
        xn = x * lax.rsqrt(_group_stat(x * x, s) + 1e-6) * g
        return xn * cos + _swap_halves(xn, HEAD_DIM // 2) * sin

    q = norm_rope(p[:, 0:GROUP_W], g_ref[0:1, :]) * (HEAD_DIM ** -0.5 * LOG2E)
    k = norm_rope(p[:, GROUP_W:2 * GROUP_W], g_ref[1:2, :])
    q_ref[0] = _bf(q)
    k_ref[0] = _bf(k)
    vt_ref[0, 0] = _bf(p[:, 2 * GROUP_W:3 * GROUP_W].T)


def _mla_prep(p, qng_ref, kvng_ref, wuq_ref, wk_ref, wv_ref, gq_ref, gk_ref, gkr_ref, place_ref, s_ref,
              cosa_ref, sina_ref, cosb_ref, sinb_ref, q_ref, k_ref, vt_ref):
    s = s_ref[...]
    half = MLA_ROPE // 2

    def rms(x, width):
        return x * lax.rsqrt(jnp.sum(x * x, axis=-1, keepdims=True) * (1.0 / width) + 1e-6)

    cqn = _bf(rms(p[:, 0:MLA_Q_RANK], MLA_Q_RANK) * qng_ref[...])
    q = _dot(cqn, wuq_ref[...])
    cosa = jnp.concatenate([cosa_ref[...]] * MLA_HEADS, axis=1)
    sina = jnp.concatenate([sina_ref[...]] * MLA_HEADS, axis=1)
    qn = q * lax.rsqrt(_group_stat(q * q, s) + 1e-6) * gq_ref[...]
    qr = (qn * cosa + _swap_halves(qn, half) * sina) * ((MLA_NOPE + MLA_ROPE) ** -0.5 * LOG2E)
    q_ref[0] = _bf(qr)

    ckvn = _bf(rms(p[:, MLA_Q_RANK:MLA_Q_RANK + MLA_KV_RANK], MLA_KV_RANK) * kvng_ref[...])
    kn = _dot(ckvn, wk_ref[...])
    knn = kn * lax.rsqrt(_group_stat(kn * kn, s) + 1e-6) * gk_ref[...]
    vt_ref[0, 0] = _bf(_dot(ckvn, wv_ref[...]).T)

    k_rope_lo = MLA_Q_RANK + MLA_KV_RANK
    krn = rms(p[:, k_rope_lo:k_rope_lo + LANES], MLA_ROPE) * gkr_ref[...]
    krr = krn * cosb_ref[...] + _swap_halves(krn, half) * sinb_ref[...]
    k_ref[0] = _bf(knn + _dot(_bf(krr), place_ref[...]))


def _attn_kernel(*refs, mode, n_keys, first_is_ctx, lam_init):
    if mode == "diff":
        q_ref, k_ref, vt_ref, lam_ref, g_ref, o_ref, s_sc = refs
    else:
        q_ref, k_ref, vt_ref, o_ref, s_sc = refs
    tq = q_ref.shape[1]
    qt = _bf(q_ref[0].astype(F32).T)
    row_lo = lax.broadcasted_iota(jnp.int32, (LANES, 1), 0) < HEAD_DIM
    if mode == "diff":
        zero = jnp.zeros_like(qt)
        qa, qb = jnp.where(row_lo, qt, zero), jnp.where(row_lo, zero, qt)
        v_dim, v_rows = LANES, (slice(None), slice(None))
    else:
        qa, qb = qt[:LANES], qt[LANES:]
        v_dim, v_rows = HEAD_DIM, (slice(0, HEAD_DIM), slice(HEAD_DIM, LANES))

    def sweep(n_iter, nb):
        def scores(c, slot):
            out = []
            rows = nb * ATTN_KB
            kc = k_ref[0, pl.ds(pl.multiple_of(c * rows, rows), rows), :]
            for half, qm in enumerate((qa, qb)):
                s = _dot(kc if mode == "diff" else kc[:, half * LANES:(half + 1) * LANES], qm)
                s_sc[slot, half, 0:rows, :] = s
                out.append(jnp.max(s, axis=0, keepdims=True))
            return tuple(out)

        def accumulate(c, slot, mxs, state):
            new = []
            for half, (mx, (m, acc)) in enumerate(zip(mxs, state)):
                m_new = jnp.maximum(m, mx)
                p = jnp.exp2(s_sc[slot, half, 0:nb * ATTN_KB, :] - m_new)
                vg = jnp.concatenate([vt_ref[0, c * nb + j, v_rows[half], :] for j in range(nb)], axis=1)
                vg = jnp.concatenate([vg, jnp.ones((ONES_ROWS, nb * ATTN_KB), BF16)], axis=0)
                acc = jnp.exp2(m - m_new) * acc + _dot(vg, _bf(p))
                new.append((m_new, acc))
            return tuple(new)

        def body(c2, carry):
            mxs, state = carry
            c = 2 * c2
            mxs1 = scores(c + 1, 1)
            state = accumulate(c, 0, mxs, state)
            mxs0 = scores(c + 2, 0)
            state = accumulate(c + 1, 1, mxs1, state)
            return mxs0, state

        init = (jnp.full((1, tq), -1e30, F32), jnp.zeros((v_dim + ONES_ROWS, tq), F32))
        n_pairs = (n_iter - 1) // 2
        mxs, state = lax.fori_loop(0, n_pairs, body, (scores(0, 0), (init, init)),
                                   unroll=2 if n_pairs % 2 == 0 and n_pairs else 1)
        (_, acca), (_, accb) = accumulate(n_iter - 1, 0, mxs, state)
        return acca, accb


    def finish(carry):
        acca, accb = carry
        oa = acca[:v_dim] / acca[v_dim:v_dim + 1]
        ob = accb[:v_dim] / accb[v_dim:v_dim + 1]
        if mode == "diff":
            lam = lam_ref[...]
            lam_full = (jnp.exp(jnp.sum(lam[0:1] * lam[1:2], axis=-1, keepdims=True))
                        - jnp.exp(jnp.sum(lam[2:3] * lam[3:4], axis=-1, keepdims=True)) + lam_init)
            o = oa - lam_full * ob
            o = o * lax.rsqrt(jnp.mean(o * o, axis=0, keepdims=True) + 1e-6) * g_ref[...] * (1.0 - lam_init)
        else:
            o = jnp.concatenate([oa, ob], axis=0)
        o_ref[0] = _bf(o.T)

    n_blocks = n_keys // ATTN_KB
    if first_is_ctx:
        i = pl.program_id(2)

        @pl.when(i == 0)
        def _():
            finish(sweep(1, CTX_LEN // ATTN_KB))

        @pl.when(i > 0)
        def _():
            finish(sweep(n_blocks // ATTN_NB, ATTN_NB))
    else:
        finish(sweep(n_blocks // ATTN_NB, ATTN_NB))


def _attention(q, k, vt, extra, *, mode, want_ctx, lam_init=0.0):
    bsz, n, _ = k.shape
    tq = ROW_TILE
    off = 0 if want_ctx else 1
    n_out = n - off * tq
    qk_w = LANES if mode == "diff" else 2 * LANES
    in_specs = [pl.BlockSpec((1, tq, qk_w), lambda b, h, i: (b, i + off, h)),
                pl.BlockSpec((1, n, qk_w), lambda b, h, i: (b, 0, h)),
                pl.BlockSpec((1, n // ATTN_KB, LANES, ATTN_KB), lambda b, h, i: (b, 0, h, 0))]
    in_specs += [pl.BlockSpec(e.shape, lambda b, h, i: (0, 0)) for e in extra]
    return pl.pallas_call(
        functools.partial(_attn_kernel, mode=mode, n_keys=n, first_is_ctx=want_ctx, lam_init=lam_init),
        grid=(bsz, GROUP_W // LANES, n_out // tq),
        in_specs=in_specs,
        out_specs=pl.BlockSpec((1, tq, LANES), lambda b, h, i: (b, i, h)),
        out_shape=jax.ShapeDtypeStruct((bsz, n_out, GROUP_W), BF16),
        scratch_shapes=[pltpu.VMEM((2, 2, ATTN_NB * ATTN_KB, tq), F32)],
        compiler_params=pltpu.CompilerParams(dimension_semantics=("parallel", "parallel", "arbitrary"),
                                             vmem_limit_bytes=VMEM_LIMIT),
        name="attn_" + mode,
    )(q, k, vt, *extra)


def _halo_valid(i, n_tiles):
    return i >= 2, (i >= 1) & (i < n_tiles - 1)


def _rwkv_prep_kernel(p_ref, prev_ref, next_ref, sw_ref, w0_ref, wup_ref, a0_ref, aup_ref, gup_ref, kkw_ref, ka_ref,
                      ssum_ref, r_ref, k_ref, v_ref, kk_ref, lw_ref, a_ref, kd_ref, g_ref):
    i = pl.program_id(1)
    has_prev, has_next = _halo_valid(i, pl.num_programs(1))
    x = p_ref[0]
    prev = jnp.where(has_prev, prev_ref[0], 0.0)
    nxt = jnp.where(has_next, next_ref[0], 0.0)
    xm1, x0, xp1 = _shifted_rows(x, prev, nxt, (1, 0, -1))
    p = sw_ref[0:1, :] * xm1 + sw_ref[1:2, :] * x0 + sw_ref[2:3, :] * xp1

    r, k, v = p[:, 0:512], p[:, 512:1024], p[:, 1024:1536]
    w_raw = w0_ref[...] + _dot(_bf(jnp.tanh(p[:, 1536:1664])), wup_ref[...])
    lw = -RWKV_DECAY_SCALE * jax.nn.sigmoid(w_raw)
    a = jax.nn.sigmoid(a0_ref[...] + _dot(_bf(p[:, 1664:1792]), aup_ref[...]))
    kk = k * kkw_ref[...]
    kk = kk / jnp.maximum(jnp.sqrt(_group_stat(kk * kk, ssum_ref[...])), 1e-12)
    k2 = jnp.concatenate([k, k], axis=1)
    ka2 = jnp.concatenate([ka_ref[...]] * 2, axis=1)
    r_ref[0] = r
    k_ref[0] = k
    v_ref[0] = v
    kk_ref[0] = kk
    lw_ref[0] = lw
    a_ref[0] = a
    kd_ref[0] = k2 * (1.0 + (a - 1.0) * ka2)
    g_ref[0] = _dot(_bf(jax.nn.sigmoid(p[:, 1792:1920])), gup_ref[...])


def _halo_specs(width, tm):
    per = tm // HALO
    return [pl.BlockSpec((1, tm, width), lambda b, i: (b, i, 0)),
            pl.BlockSpec((1, HALO, width), lambda b, i: (b, jnp.maximum(i * per - 1, 0), 0)),
            pl.BlockSpec((1, HALO, width), lambda b, i: (b, jnp.minimum((i + 1) * per, pl.num_programs(1) * per - 1), 0))]


def _rwkv_prep(p, consts):
    bsz, n, _ = p.shape
    tm = ROW_TILE
    n_t = n // tm
    per = tm // HALO
    row = lambda b, i: (b, i, 0)
    const = lambda b, i: (0, 0)
    in_specs = [pl.BlockSpec((1, tm, P_RWKV), row),
                pl.BlockSpec((1, HALO, P_RWKV), lambda b, i: (b, jnp.maximum(i * per - 1, 0), 0)),
                pl.BlockSpec((1, HALO, P_RWKV), lambda b, i: (b, jnp.minimum((i + 1) * per, n_t * per - 1), 0))]
    in_specs += [pl.BlockSpec(c.shape, const) for c in consts]
    widths = (512, 512, 512, 512, 1024, 1024, 1024, 512)
    return pl.pallas_call(
        _rwkv_prep_kernel,
        grid=(bsz, n_t),
        in_specs=in_specs,
        out_specs=[pl.BlockSpec((1, tm, w), row) for w in widths],
        out_shape=[jax.ShapeDtypeStruct((bsz, n, w), F32) for w in widths],
        compiler_params=pltpu.CompilerParams(dimension_semantics=("parallel", "parallel"),
                                             vmem_limit_bytes=VMEM_LIMIT),
        name="rwkv_prep",
    )(p, p, p, *consts)


def _rwkv_scan_kernel(*refs, bsz):
    c = RWKV_CHUNK
    ins, y_refs, gt_sc = refs[:12], refs[12:14], refs[14]

    @pl.when(pl.program_id(0) == 0)
    def _():
        gt_sc[...] = jnp.zeros_like(gt_sc)

    ti = lax.broadcasted_iota(jnp.int32, (c, c), 0)
    si = lax.broadcasted_iota(jnp.int32, (c, c), 1)
    t2 = lax.broadcasted_iota(jnp.int32, (2 * c, 2 * c), 0)
    s2 = lax.broadcasted_iota(jnp.int32, (2 * c, 2 * c), 1)
    same_head = (t2 & c) == (s2 & c)
    eye = jnp.where(t2 == s2, 1.0, 0.0)
    m0 = lax.broadcasted_iota(jnp.int32, (1, LANES), 1) < HEAD_DIM

    def masks(fwd):
        tri = jnp.where((ti - si if fwd else si - ti) >= 0, 1.0, 0.0).astype(BF16)
        lag = jnp.where(same_head, t2 - s2 if fwd else s2 - t2, -1)
        return tri, lag > 0, lag >= 0

    def stack(x):
        return jnp.concatenate([jnp.where(m0, x, 0.0), jnp.where(m0, 0.0, x)], axis=0)

    chains = []
    for d in range(2):
        r_ref, kk_ref, v_ref, lw_ref, a_ref, kd_ref = ins[6 * d:6 * d + 6]
        tri, strict, incl = masks(d == 0)
        for b in range(bsz):
            for p in range(GROUP_W // LANES):
                sl = slice(p * LANES, (p + 1) * LANES)
                lw = lw_ref[b, :, sl]
                cum = _dot_exact_lhs(tri, lw)
                e_in, e_ex, e_neg = jnp.exp(cum), jnp.exp(cum - lw), jnp.exp(-cum)
                pe = jnp.exp(cum[c - 1:c, :] if d == 0 else cum[0:1, :])
                kk = kk_ref[b, :, sl]
                rb = stack(-(kk * a_ref[b, :, sl]) * e_neg)
                rk = stack(kd_ref[b, :, sl] * e_neg)
                vs = stack(v_ref[b, :, sl])
                chains.append(dict(
                    strict=strict, incl=incl, p_end=pe, vs=vs, vs_b=_bf(vs),
                    la=_bf(stack(kk * e_ex)), lr=_bf(stack(r_ref[b, :, sl] * e_in)), rb=_bf(rb), rk=_bf(rk),
                    rr=_bf(jnp.concatenate([rb * pe, rk * pe], axis=0)),
                    state=(d * bsz + b) * (GROUP_W // LANES) + p, out=(y_refs[d], b, sl)))

    two = 2 * c
    for ch in chains:
        rbk = jnp.concatenate([ch["rb"], ch["rk"]], axis=0)
        from_a, from_r = _dot_nt(ch["la"], rbk), _dot_nt(ch["lr"], rbk)
        ch["a_ab"] = jnp.where(ch["strict"], from_a[:, :two], 0.0)
        ch["a_ak"] = _bf(jnp.where(ch["strict"], from_a[:, two:], 0.0))
        ch["a_rb"] = _bf(jnp.where(ch["incl"], from_r[:, :two], 0.0))
        ch["a_rk"] = _bf(jnp.where(ch["incl"], from_r[:, two:], 0.0))

    for ch in chains:
        ch["tm"] = eye + ch["a_ab"]
        ch["apow"] = ch["a_ab"]
    for _ in range(5):
        for ch in chains:
            ap = _bf(ch["apow"])
            ch["apow"] = _dot(ap, ap)
        for ch in chains:
            ch["tm"] = ch["tm"] + _dot(_bf(ch["tm"]), _bf(ch["apow"]))
    for ch in chains:
        ch["tm"] = _bf(ch["tm"])
        ch["akv"] = _bf(_dot(ch["a_ak"], ch["vs_b"]))
    for ch in chains:
        solved = _dot(ch["tm"], jnp.concatenate([ch["la"], ch["akv"]], axis=1))
        ch["w"] = _bf(solved[:, :LANES])
        ch["u0"] = solved[:, LANES:]
        ch["y0"] = _dot(ch["a_rk"], ch["vs_b"])

    for ch in chains:
        ch["gt"] = gt_sc[ch["state"]]
        ch["gt_b"] = _bf(ch["gt"])
        ch["u"] = _dot_nt(ch["w"], ch["gt_b"]) + ch["u0"]
    for ch in chains:
        y = _dot_nt(ch["lr"], ch["gt_b"]) + _dot(ch["a_rb"], _bf(ch["u"])) + ch["y0"]
        y_ref, b, sl = ch["out"]
        y_ref[b, :, sl] = y[:c] + y[c:]
        uv = jnp.concatenate([ch["u"], ch["vs"]], axis=0)
        gt_sc[ch["state"]] = ch["gt"] * ch["p_end"] + _dot(_bf(uv.T), ch["rr"])


def _bwd_chunk_index(s, n_ctx, n_all):
    return jnp.where(s < n_ctx, n_ctx - 1 - s, n_all + n_ctx - 1 - s)


def _rwkv_scan(r, kk, v, lw, a, kd):
    bsz, n, _ = r.shape
    c = RWKV_CHUNK
    n_all, n_ctx = n // c, CTX_LEN // c
    blk = (bsz, c, GROUP_W)
    fwd = [pl.BlockSpec(blk, lambda s: (0, s, 0))] * 6
    bwd = ([pl.BlockSpec(blk, lambda s: (0, _bwd_chunk_index(s, n_ctx, n_all), 0))] * 3
           + [pl.BlockSpec(blk, lambda s: (0, _bwd_chunk_index(s, n_ctx, n_all), 1))] * 3)
    return pl.pallas_call(
        functools.partial(_rwkv_scan_kernel, bsz=bsz),
        grid=(n_all,),
        in_specs=fwd + bwd,
        out_specs=[pl.BlockSpec(blk, lambda s: (0, s, 0)),
                   pl.BlockSpec(blk, lambda s: (0, _bwd_chunk_index(s, n_ctx, n_all), 0))],
        out_shape=[jax.ShapeDtypeStruct((bsz, n, GROUP_W), F32)] * 2,
        scratch_shapes=[pltpu.VMEM((2 * bsz * (GROUP_W // LANES), LANES, LANES), F32)],
        compiler_params=pltpu.CompilerParams(dimension_semantics=("arbitrary",), vmem_limit_bytes=VMEM_LIMIT),
        name="rwkv_scan",
    )(r, kk, v, lw, a, kd, r, kk, v, lw, a, kd)


def _ssd_prep_kernel(p_ref, prev_ref, next_ref, cw_ref, cb_ref, dtb_ref, aneg_ref, e_ref, pa_ref,
                     xs_ref, bc_ref, xdt_ref, adt_ref):
    i = pl.program_id(1)
    has_prev, has_next = _halo_valid(i, pl.num_programs(1))
    x = p_ref[0, :, 512:1536]
    prev = jnp.where(has_prev, prev_ref[0, :, 512:1536], 0.0)
    nxt = jnp.where(has_next, next_ref[0, :, 512:1536], 0.0)
    taps = _shifted_rows(x, prev, nxt, (2, 1, 0, -1, -2))
    conv = cb_ref[...]
    for j, tap in enumerate(taps):
        conv = conv + cw_ref[j:j + 1, :] * tap
    xbc = _silu(conv)
    xs = xbc[:, 0:512]
    xs_ref[0] = xs
    bc_ref[0] = xbc[:, 512:1024]
    t = p_ref[0, :, 1536:1664] + dtb_ref[...]
    dt = jnp.maximum(t, 0.0) + jnp.log(1.0 + jnp.exp(-jnp.abs(t)))
    xdt_ref[0] = jnp.concatenate([xs, xs], axis=1) * _dot_exact_rhs(dt, e_ref[...])
    adt_ref[0] = _dot_exact_rhs(dt * aneg_ref[...], pa_ref[...])


def _ssd_prep(p, consts):
    bsz, n, _ = p.shape
    tm = ROW_TILE
    n_t = n // tm
    per = tm // HALO
    row = lambda b, i: (b, i, 0)
    const = lambda b, i: (0, 0)
    in_specs = [pl.BlockSpec((1, tm, P_SSD), row),
                pl.BlockSpec((1, HALO, P_SSD), lambda b, i: (b, jnp.maximum(i * per - 1, 0), 0)),
                pl.BlockSpec((1, HALO, P_SSD), lambda b, i: (b, jnp.minimum((i + 1) * per, n_t * per - 1), 0))]
    in_specs += [pl.BlockSpec(c.shape, const) for c in consts]
    widths = (512, 512, 1024, 256)
    return pl.pallas_call(
        _ssd_prep_kernel,
        grid=(bsz, n_t),
        in_specs=in_specs,
        out_specs=[pl.BlockSpec((1, tm, w), row) for w in widths],
        out_shape=[jax.ShapeDtypeStruct((bsz, n, w), F32) for w in widths],
        compiler_params=pltpu.CompilerParams(dimension_semantics=("parallel", "parallel"),
                                             vmem_limit_bytes=VMEM_LIMIT),
        name="ssd_prep",
    )(p, p, p, *consts)


def _ssd_scan_kernel(*refs, bsz):
    q = SSD_CHUNK
    ins, y_refs, ht_sc = refs[:6], refs[6:8], refs[8]

    @pl.when(pl.program_id(0) == 0)
    def _():
        ht_sc[...] = jnp.zeros_like(ht_sc)

    ti = lax.broadcasted_iota(jnp.int32, (q, q), 0)
    si = lax.broadcasted_iota(jnp.int32, (q, q), 1)
    m0 = lax.broadcasted_iota(jnp.int32, (1, LANES), 1) < HEAD_DIM

    def per_head(fn, h0):
        return jnp.where(m0, fn(h0), fn(h0 + 1))

    for d in range(2):
        x_ref, a_ref, bc_ref = ins[3 * d:3 * d + 3]
        tri = (ti - si if d == 0 else si - ti) >= 0
        tri_b = jnp.where(tri, 1.0, 0.0).astype(BF16)
        for b in range(bsz):
            acum = _dot_exact_lhs(tri_b, a_ref[b])
            acum_t = acum.T
            last = acum[q - 1:q, :] if d == 0 else acum[0:1, :]
            bc = bc_ref[b]

            def decay(h, acum=acum, acum_t=acum_t, tri=tri):
                seg = acum[:, h:h + 1] - acum_t[h:h + 1, :]
                return jnp.where(tri, jnp.exp(jnp.where(tri, seg, 0.0)), 0.0)

            for g in range(2):
                bg = bc[:, g * SSD_STATE:(g + 1) * SSD_STATE]
                cg_b = _bf(bc[:, (2 + g) * SSD_STATE:(3 + g) * SSD_STATE])
                scores = _dot_nt(cg_b, _bf(bg))
                bg_t = _bf(bg.T)
                for pp in range(2):
                    pair = 2 * g + pp
                    h0 = 2 * pair
                    state = (d * bsz + b) * (GROUP_W // LANES) + pair
                    sl = slice(pair * LANES, (pair + 1) * LANES)
                    xp = x_ref[b, :, sl]
                    xp_b = _bf(xp)
                    y0 = _dot(_bf(scores * decay(h0)), xp_b)
                    y1 = _dot(_bf(scores * decay(h0 + 1)), xp_b)
                    ht = ht_sc[state]
                    carry = _dot(cg_b, _bf(ht)) * per_head(lambda h: jnp.exp(acum[:, h:h + 1]), h0)
                    y_refs[d][b, :, sl] = jnp.where(m0, y0, y1) + carry
                    to_end = per_head(lambda h: jnp.exp(last[:, h:h + 1] - acum[:, h:h + 1]), h0)
                    e_last = per_head(lambda h: jnp.exp(last[:, h:h + 1]), h0)
                    ht_sc[state] = ht * e_last + _dot(bg_t, _bf(xp * to_end))


def _ssd_scan(xdt, adt, bc):
    bsz, n, _ = bc.shape
    q = SSD_CHUNK
    n_all, n_ctx = n // q, CTX_LEN // q
    bwd = lambda s: _bwd_chunk_index(s, n_ctx, n_all)
    wide, narrow = (bsz, q, GROUP_W), (bsz, q, LANES)
    in_specs = [pl.BlockSpec(wide, lambda s: (0, s, 0)), pl.BlockSpec(narrow, lambda s: (0, s, 0)),
                pl.BlockSpec(wide, lambda s: (0, s, 0)),
                pl.BlockSpec(wide, lambda s: (0, bwd(s), 1)), pl.BlockSpec(narrow, lambda s: (0, bwd(s), 1)),
                pl.BlockSpec(wide, lambda s: (0, bwd(s), 0))]
    return pl.pallas_call(
        functools.partial(_ssd_scan_kernel, bsz=bsz),
        grid=(n_all,),
        in_specs=in_specs,
        out_specs=[pl.BlockSpec(wide, lambda s: (0, s, 0)), pl.BlockSpec(wide, lambda s: (0, bwd(s), 0))],
        out_shape=[jax.ShapeDtypeStruct((bsz, n, GROUP_W), F32)] * 2,
        scratch_shapes=[pltpu.VMEM((2 * bsz * (GROUP_W // LANES), SSD_STATE, LANES), F32)],
        compiler_params=pltpu.CompilerParams(dimension_semantics=("arbitrary",), vmem_limit_bytes=VMEM_LIMIT),
        name="ssd_scan",
    )(xdt, adt, bc, xdt, adt, bc)


def _post_kernel(z_ref, mctx_ref, mlat_ref, od_ref, om_ref, yrf_ref, yrb_ref, r_ref, k_ref, v_ref, g_ref,
                 lng_ref, lnb_ref, rk_ref, ysf_ref, ysb_ref, xs_ref, zg_ref, dsk_ref, ng_ref, smean_ref, ssum_ref,
                 w_ref, o_ref, *, tile_off):
    is_ctx = _is_ctx_rows(pl.program_id(1) + tile_off, ROW_TILE, CTX_LEN)
    smean = smean_ref[...]
    y = yrf_ref[0] + yrb_ref[0]
    yc = y - _group_stat(y, smean)
    yn = yc * lax.rsqrt(_group_stat(yc * yc, smean) + RWKV_LN_EPS) * lng_ref[...] + lnb_ref[...]
    bonus = _group_stat(r_ref[0] * k_ref[0] * rk_ref[...], ssum_ref[...]) * v_ref[0]
    o_rwkv = (yn + bonus) * g_ref[0]

    t = (ysf_ref[0] + ysb_ref[0] + dsk_ref[...] * xs_ref[0]) * _silu(zg_ref[0])
    o_ssd = t * lax.rsqrt(jnp.mean(t * t, axis=-1, keepdims=True) + 1e-6) * ng_ref[...]

    mix = (_dot(od_ref[0], w_ref[0:512, :]) + _dot(_bf(o_rwkv), w_ref[512:1024, :])
           + _dot(om_ref[0], w_ref[1024:1536, :]) + _dot(_bf(o_ssd), w_ref[1536:2048, :]))
    o_ref[0] = z_ref[0] + _mod_row(mctx_ref, mlat_ref, 5, is_ctx) * mix


def _post(z, m, od, om, yr, r, k, v, g, ys, xs, p_ssd, consts, w, *, want_ctx):
    bsz, n, d = z.shape
    tm = ROW_TILE
    off = 0 if want_ctx else 1
    n_out = n - off * tm
    full = lambda b, i: (b, i + off, 0)
    attn = lambda b, i: (b, i, 0)
    const = lambda b, i: (0, 0)
    seq = lambda width: pl.BlockSpec((1, tm, width), full)
    in_specs = [seq(d),
                pl.BlockSpec((1, N_MOD, d), lambda b, i: (0, 0, 0)),
                pl.BlockSpec((1, N_MOD, d), lambda b, i: (b + 1, 0, 0)),
                pl.BlockSpec((1, tm, GROUP_W), attn), pl.BlockSpec((1, tm, GROUP_W), attn),
                seq(GROUP_W), seq(GROUP_W), seq(GROUP_W), seq(GROUP_W), seq(GROUP_W), seq(GROUP_W)]
    in_specs += [pl.BlockSpec(c.shape, const) for c in consts[:3]]
    in_specs += [seq(GROUP_W), seq(GROUP_W), seq(GROUP_W), seq(GROUP_W)]
    in_specs += [pl.BlockSpec(c.shape, const) for c in consts[3:]]
    in_specs += [pl.BlockSpec(w.shape, const, pipeline_mode=pl.Buffered(1))]
    return pl.pallas_call(
        functools.partial(_post_kernel, tile_off=off),
        grid=(bsz, n_out // tm),
        in_specs=in_specs,
        out_specs=pl.BlockSpec((1, tm, d), lambda b, i: (b, i, 0)),
        out_shape=jax.ShapeDtypeStruct((bsz, n_out, d), F32),
        compiler_params=pltpu.CompilerParams(dimension_semantics=("parallel", "parallel"),
                                             vmem_limit_bytes=VMEM_LIMIT),
        name="post",
    )(z, m, m, od, om, yr[0], yr[1], r, k, v, g, *consts[:3], ys[0], ys[1], xs, p_ssd, *consts[3:], w)


def _pad_cols(w, width):
    return jnp.pad(w, ((0, 0), (0, width - w.shape[1])))


def _block_diag_const(block, reps):
    return jnp.asarray(np.kron(np.eye(reps, dtype=np.float32), block), BF16)


def _group_consts():
    ones64 = np.ones((64, 64), np.float32)
    mla_blk = np.zeros((LANES, LANES), np.float32)
    mla_blk[:MLA_NOPE, :MLA_NOPE] = 1.0 / MLA_NOPE
    mla_blk[MLA_NOPE:MLA_NOPE + MLA_ROPE, MLA_NOPE:MLA_NOPE + MLA_ROPE] = 1.0 / MLA_ROPE
    return (_block_diag_const(ones64 / 64.0, 4), _block_diag_const(ones64, 4), _block_diag_const(mla_blk, 2))


def _rope_angles(n_lat, dim):
    quarter = dim // 4
    inv = ROPE_BASE ** (-jnp.arange(quarter, dtype=F32) / quarter)
    rows = n_lat // GRID_W
    pos_r = jnp.repeat(jnp.arange(rows), GRID_W).astype(F32)
    pos_c = jnp.tile(jnp.arange(GRID_W), rows).astype(F32)
    ang = jnp.concatenate([pos_r[:, None] * inv, pos_c[:, None] * inv], axis=-1)
    return jnp.cos(ang), jnp.sin(ang)


def _rope_lane_tables(n_lat, dim, lane_off):
    cos, sin = _rope_angles(n_lat, dim)
    cos_t = jnp.ones((n_lat, LANES), F32).at[:, lane_off:lane_off + dim].set(jnp.concatenate([cos, cos], axis=-1))
    sin_t = jnp.zeros((n_lat, LANES), F32).at[:, lane_off:lane_off + dim].set(jnp.concatenate([-sin, sin], axis=-1))
    cos_t = jnp.concatenate([jnp.ones((CTX_LEN, LANES), F32), cos_t], axis=0)
    sin_t = jnp.concatenate([jnp.zeros((CTX_LEN, LANES), F32), sin_t], axis=0)
    return cos_t, sin_t


def _row(x):
    return x.reshape(1, -1).astype(F32)


def _mla_consts(q_norm_g, kv_norm_g, w_uq, w_ukv, nope_g, rope_g, s_mla):
    hq = MLA_NOPE + MLA_ROPE
    w_uq = w_uq.reshape(MLA_Q_RANK, MLA_HEADS, hq)
    w_uq = jnp.pad(w_uq, ((0, 0), (0, 0), (0, LANES - hq))).reshape(MLA_Q_RANK, MLA_HEADS * LANES)
    w_ukv = w_ukv.reshape(MLA_KV_RANK, MLA_HEADS, 2 * MLA_NOPE)
    w_k = jnp.pad(w_ukv[:, :, :MLA_NOPE], ((0, 0), (0, 0), (0, LANES - MLA_NOPE))).reshape(MLA_KV_RANK, -1)
    w_v = w_ukv[:, :, MLA_NOPE:].reshape(MLA_KV_RANK, GROUP_W)
    zpad = jnp.zeros((LANES - hq,), F32)
    g_q = jnp.tile(jnp.concatenate([nope_g[0], rope_g[0], zpad]), MLA_HEADS)
    g_k = jnp.tile(jnp.concatenate([nope_g[1], jnp.zeros((LANES - MLA_NOPE,), F32)]), MLA_HEADS)
    g_kr = jnp.concatenate([rope_g[1], jnp.zeros((LANES - MLA_ROPE,), F32)])
    place = np.zeros((LANES, MLA_HEADS * LANES), np.float32)
    for h in range(MLA_HEADS):
        place[np.arange(MLA_ROPE), h * LANES + MLA_NOPE + np.arange(MLA_ROPE)] = 1.0
    return (_row(q_norm_g), _row(kv_norm_g), _bf(w_uq), _bf(w_k), _bf(w_v), _row(g_q), _row(g_k), _row(g_kr),
            jnp.asarray(place, BF16), s_mla)


def _two_dir_block_diag(w):
    z = jnp.zeros_like(w[0])
    return jnp.concatenate([jnp.concatenate([w[0], z], axis=1), jnp.concatenate([z, w[1]], axis=1)], axis=0)


def _ssd_expand_consts():
    e = np.zeros((LANES, 2 * GROUP_W), np.float32)
    pa = np.zeros((LANES, 2 * LANES), np.float32)
    for d in range(2):
        for h in range(SSD_HEADS):
            e[d * SSD_HEADS + h, d * GROUP_W + h * HEAD_DIM:d * GROUP_W + (h + 1) * HEAD_DIM] = 1.0
            pa[d * SSD_HEADS + h, d * LANES + h] = 1.0
    return jnp.asarray(e, BF16), jnp.asarray(pa, BF16)


def _pad_lanes(v):
    return jnp.pad(v.reshape(1, -1).astype(F32), ((0, 0), (0, LANES - v.size)))


def kernel(x, c, ctx, c_ctx, mod_w, mod_b, norm_g, ffn_w_gate, ffn_w_up, ffn_w_down, w_in, w_out, diff_qk_g, diff_lambda, diff_subln_g, rwkv_shift_w, rwkv_w0, rwkv_w_up, rwkv_a0, rwkv_a_up, rwkv_g_up, rwkv_k_k, rwkv_k_a, rwkv_r_k, rwkv_ln_g, rwkv_ln_b, mla_q_norm_g, mla_kv_norm_g, mla_w_uq, mla_w_ukv, mla_nope_g, mla_rope_g, ssd_conv_w, ssd_conv_b, ssd_dt_bias, ssd_a_log, ssd_d, ssd_norm_g):
    bsz, n_lat, d = x.shape
    depth = mod_w.shape[0]
    s_mean64, s_sum64, s_mla = _group_consts()
    e_dt, p_adt = _ssd_expand_consts()
    cos_d, sin_d = _rope_lane_tables(n_lat, HEAD_DIM, 0)
    cos_d = cos_d.at[:, HEAD_DIM:].set(cos_d[:, :HEAD_DIM])
    sin_d = sin_d.at[:, HEAD_DIM:].set(sin_d[:, :HEAD_DIM])
    cos_ma, sin_ma = _rope_lane_tables(n_lat, MLA_ROPE, MLA_NOPE)
    cos_mb, sin_mb = _rope_lane_tables(n_lat, MLA_ROPE, 0)

    cvec = jnp.zeros((8, d), F32).at[0].set(c_ctx).at[1:1 + bsz].set(c)
    z = jnp.concatenate([ctx, x], axis=1)
    tm_ffn = 640
    wg_all, wu_all, wd_all = _bf(ffn_w_gate), _bf(ffn_w_up), _bf(ffn_w_down)

    for l in range(depth):
        want_ctx = l < depth - 1
        lam_init = 0.8 - 0.6 * float(np.exp(-0.3 * l))
        m = _mod(cvec, mod_w, mod_b.reshape(depth, 1, N_MOD * d), layer=l).reshape(8, N_MOD, d)

        z = _ffn(z, m, _row(norm_g[l, 0]), wg_all, wu_all, wd_all, layer=l, which=0,
                 j=0, ctx_len=CTX_LEN, tm=tm_ffn)

        w = w_in[l]
        w_p = jnp.concatenate([w[:, :3456], _pad_cols(w[:, 3456:4000], P_MLA), _pad_cols(w[:, 4000:], P_SSD)], axis=1)
        qd, kd, vd, p_rwkv, qm, km, vm, p_ssd = _inproj(
            z, m, _row(norm_g[l, 1]), _bf(w_p),
            (jnp.tile(diff_qk_g[l], (1, 2 * DIFF_HEADS)), s_mean64), (cos_d, sin_d),
            _mla_consts(mla_q_norm_g[l], mla_kv_norm_g[l], mla_w_uq[l], mla_w_ukv[l], mla_nope_g[l], mla_rope_g[l],
                        s_mla),
            (cos_ma, sin_ma, cos_mb, sin_mb))
        o_diff = _attention(qd, kd, vd, (diff_lambda[l], diff_subln_g[l].reshape(-1, 1)), mode="diff", want_ctx=want_ctx,
                            lam_init=lam_init)
        o_mla = _attention(qm, km, vm, (), mode="mla", want_ctx=want_ctx)

        rw_consts = (rwkv_shift_w[l], _row(rwkv_w0[l]), _bf(_two_dir_block_diag(rwkv_w_up[l])), _row(rwkv_a0[l]),
                     _bf(_two_dir_block_diag(rwkv_a_up[l])), _bf(rwkv_g_up[l]), _row(rwkv_k_k[l]),
                     _row(rwkv_k_a[l]), s_sum64)
        r, k, v, kk, lw, a, kdir, g = _rwkv_prep(p_rwkv, rw_consts)
        y_rwkv = _rwkv_scan(r, kk, v, lw, a, kdir)

        sd_consts = (ssd_conv_w[l], _row(ssd_conv_b[l]), _pad_lanes(ssd_dt_bias[l]),
                     _pad_lanes(-jnp.exp(ssd_a_log[l])), e_dt, p_adt)
        xs, bc, xdt, adt = _ssd_prep(p_ssd, sd_consts)
        y_ssd = _ssd_scan(xdt, adt, bc)

        post_consts = (_row(rwkv_ln_g[l]), _row(rwkv_ln_b[l]), _row(rwkv_r_k[l]),
                       _row(jnp.repeat(ssd_d[l], HEAD_DIM)), _row(ssd_norm_g[l]), s_mean64, s_sum64)
        z = _post(z, m, o_diff, o_mla, y_rwkv, r, k, v, g, y_ssd, xs, p_ssd, post_consts, _bf(w_out[l]),
                  want_ctx=want_ctx)

        z = _ffn(z, m, _row(norm_g[l, 2]), wg_all, wu_all, wd_all, layer=l, which=1,
                 j=2, ctx_len=CTX_LEN if want_ctx else 0, tm=tm_ffn if want_ctx else 512)
    return z
```
